```python
import jax, jax.numpy as jnp
from jax import lax
import numpy as np

D_MODEL = 2048
BATCH = 2
SEQ = 4096
DEPTH = 4

HEAD_DIM = 128
N_HEADS_TOTAL = D_MODEL // HEAD_DIM
N_HEADS_FOX = N_HEADS_TOTAL // 2
N_HEADS_SB = N_HEADS_TOTAL - N_HEADS_FOX
N_HEADS_MLA = N_HEADS_TOTAL // 2
N_HEADS_DIL = N_HEADS_TOTAL - N_HEADS_MLA
Q_LORA_RANK = 512
KV_LORA_RANK = 512
QK_NOPE_DIM = 128
QK_ROPE_DIM = 64
V_HEAD_DIM = 128
ROPE_THETA = 10000.0
DIL_WINDOWS = (128, 512, 2048)
DIL_RATES = (1, 4, 16)
DIL_STEPS = 128
Q_BLOCK = 128
D_FF_DENSE = 5632
N_EXPERTS = 8
TOP_K = 2
D_FF_EXPERT = 7168
NORM_EPS = 1e-6

FOX_W = N_HEADS_FOX * HEAD_DIM
SB_W = N_HEADS_SB * HEAD_DIM
EVEN_IN = 3 * FOX_W + 3 * SB_W + N_HEADS_FOX
EVEN_OUT = FOX_W + SB_W
MLA_IN = Q_LORA_RANK + KV_LORA_RANK + QK_ROPE_DIM
DIL_W = N_HEADS_DIL * HEAD_DIM
ODD_IN = MLA_IN + 3 * DIL_W
ODD_OUT = N_HEADS_MLA * V_HEAD_DIM + DIL_W

kernel_name = 'hybrid_fox_sb_mla_dilated_moe'


def rmsnorm(x, g):
    xf = x.astype(jnp.float32)
    y = xf * lax.rsqrt(jnp.mean(xf * xf, axis=-1, keepdims=True) + NORM_EPS)
    return (y * g.astype(jnp.float32)).astype(x.dtype)


def split_cols(p, sizes):
    out, off = [], 0
    for n in sizes:
        out.append(p[..., off:off + n])
        off += n
    return out


def to_blocks(t):
    b, s = t.shape[:2]
    return jnp.moveaxis(t.reshape((b, s // Q_BLOCK, Q_BLOCK) + t.shape[2:]), 1, 0)


def from_blocks(t):
    t = jnp.moveaxis(t, 0, 1)
    return t.reshape((t.shape[0], t.shape[1] * t.shape[2]) + t.shape[3:])


def alibi_slopes(n):
    return 2.0 ** (-8.0 * jnp.arange(1, n + 1, dtype=jnp.float32) / n)


def apply_rope(x, pos):
    half = x.shape[-1] // 2
    inv_freq = ROPE_THETA ** (-jnp.arange(half, dtype=jnp.float32) / half)
    ang = pos.astype(jnp.float32)[:, None] * inv_freq[None, :]
    cos, sin = jnp.cos(ang)[:, None, :], jnp.sin(ang)[:, None, :]
    x1, x2 = x[..., :half].astype(jnp.float32), x[..., half:].astype(jnp.float32)
    return jnp.concatenate([x1 * cos - x2 * sin, x2 * cos + x1 * sin], axis=-1).astype(x.dtype)


def blocked_causal_softmax(q, k, v, scale, decay_cum=None):
    s = q.shape[1]
    pos = jnp.arange(s)

    def block(qb, qpos, bias):
        logits = jnp.einsum('bqhd,bkhd->bhqk', qb, k).astype(jnp.float32) * scale
        if bias is not None:
            logits = logits + bias
        logits = jnp.where(pos[None, :] <= qpos[:, None], logits, -jnp.inf)
        p = jax.nn.softmax(logits, axis=-1)
        return jnp.einsum('bhqk,bkhd->bqhd', p.astype(v.dtype), v)

    qpos_blocks = pos.reshape(-1, Q_BLOCK)
    if decay_cum is None:
        out = lax.map(lambda a: block(a[0], a[1], None), (to_blocks(q), qpos_blocks))
    else:
        c_k = jnp.transpose(decay_cum, (0, 2, 1))
        out = lax.map(
            lambda a: block(a[0], a[1], jnp.transpose(a[2], (0, 2, 1))[..., None] - c_k[:, :, None, :]),
            (to_blocks(q), qpos_blocks, to_blocks(decay_cum)))
    return from_blocks(out)


def stick_breaking_attention(q, k, v):
    s, dh = q.shape[1], q.shape[-1]
    pos = jnp.arange(s)

    def block(args):
        qb, qpos = args
        z = jnp.einsum('bqhd,bkhd->bhqk', qb, k).astype(jnp.float32) * dh ** -0.5
        strict = pos[None, :] < qpos[:, None]
        log_1m = jnp.where(strict, jax.nn.log_sigmoid(-z), 0.0)
        after = lax.cumsum(log_1m, axis=3, reverse=True) - log_1m
        a = jnp.where(strict, jnp.exp(jax.nn.log_sigmoid(z) + after), 0.0)
        return jnp.einsum('bhqk,bkhd->bqhd', a.astype(v.dtype), v)

    return from_blocks(lax.map(block, (to_blocks(q), pos.reshape(-1, Q_BLOCK))))


def dilated_branch(q, k, v, rate, slopes):
    b, s, h, dh = q.shape
    L = s // rate
    nb = -(-L // Q_BLOCK)
    Lp = nb * Q_BLOCK

    def split(t):
        t = t.reshape(b, L, rate, h, dh).transpose(0, 2, 1, 3, 4).reshape(b * rate, L, h, dh)
        return jnp.pad(t, ((0, 0), (0, Lp - L), (0, 0), (0, 0))).reshape(b * rate, nb, Q_BLOCK, h, dh)

    def with_prev(t):
        prev = jnp.pad(t[:, :-1], ((0, 0), (1, 0), (0, 0), (0, 0), (0, 0)))
        return jnp.concatenate([prev, t], axis=2)

    qb, kk, vv = split(q), with_prev(split(k)), with_prev(split(v))
    logits = jnp.einsum('nbqhd,nbkhd->nbhqk', qb, kk).astype(jnp.float32) * dh ** -0.5
    qi = jnp.arange(Q_BLOCK)[:, None]
    ki = jnp.arange(2 * Q_BLOCK)[None, :]
    steps = Q_BLOCK + qi - ki
    key_idx = (jnp.arange(nb)[:, None, None] - 1) * Q_BLOCK + ki[None]
    valid = (steps >= 0) & (steps <= DIL_STEPS) & (key_idx >= 0)
    alibi = -slopes[:, None, None] * (steps * rate).astype(jnp.float32)
    logits = jnp.where(valid[None, :, None], logits + alibi[None, None], -jnp.inf)
    lse = jax.nn.logsumexp(logits, axis=-1)
    p = jnp.exp(logits - lse[..., None])
    out = jnp.einsum('nbhqk,nbkhd->nbqhd', p.astype(v.dtype), vv)
    out = out.reshape(b * rate, Lp, h, dh)[:, :L].reshape(b, rate, L, h, dh)
    out = out.transpose(0, 2, 1, 3, 4).reshape(b, s, h, dh)
    lse = jnp.transpose(lse, (0, 1, 3, 2)).reshape(b * rate, Lp, h)[:, :L].reshape(b, rate, L, h)
    lse = lse.transpose(0, 2, 1, 3).reshape(b, s, h)
    return out, lse


def dilated_attention(q, k, v):
    slopes = alibi_slopes(q.shape[2])
    res = [dilated_branch(q, k, v, r, slopes) for r in DIL_RATES]
    outs = jnp.stack([o for o, _ in res], axis=0)
    lses = jnp.stack([l for _, l in res], axis=0)
    w = jax.nn.softmax(lses, axis=0)
    return jnp.einsum('gbsh,gbshd->bshd', w, outs.astype(jnp.float32)).astype(q.dtype)


def even_mixer(h, w_in, b_f, w_o):
    b, s, _ = h.shape
    qf, kf, vf, qs, ks, vs, fg = split_cols(h @ w_in, [FOX_W] * 3 + [SB_W] * 3 + [N_HEADS_FOX])
    fox = lambda t: t.reshape(b, s, N_HEADS_FOX, HEAD_DIM)
    sb = lambda t: t.reshape(b, s, N_HEADS_SB, HEAD_DIM)
    log_f = jax.nn.log_sigmoid((fg + b_f).astype(jnp.float32))
    c = jnp.cumsum(log_f, axis=1)
    o_a = blocked_causal_softmax(fox(qf), fox(kf), fox(vf), HEAD_DIM ** -0.5, c)
    o_b = stick_breaking_attention(sb(qs), sb(ks), sb(vs))
    o = jnp.concatenate([o_a.reshape(b, s, FOX_W), o_b.reshape(b, s, SB_W)], axis=-1)
    return o @ w_o


def odd_mixer(h, w_in, g_cq, g_ckv, w_uq, w_ukv, w_o, pos):
    b, s, _ = h.shape
    c_q, c_kv, k_rope, qd, kd, vd = split_cols(
        h @ w_in, [Q_LORA_RANK, KV_LORA_RANK, QK_ROPE_DIM, DIL_W, DIL_W, DIL_W])
    q = (rmsnorm(c_q, g_cq) @ w_uq).reshape(b, s, N_HEADS_MLA, QK_NOPE_DIM + QK_ROPE_DIM)
    q = jnp.concatenate([q[..., :QK_NOPE_DIM], apply_rope(q[..., QK_NOPE_DIM:], pos)], axis=-1)
    kv = (rmsnorm(c_kv, g_ckv) @ w_ukv).reshape(b, s, N_HEADS_MLA, QK_NOPE_DIM + V_HEAD_DIM)
    k_pe = jnp.broadcast_to(apply_rope(k_rope[:, :, None, :], pos), (b, s, N_HEADS_MLA, QK_ROPE_DIM))
    k = jnp.concatenate([kv[..., :QK_NOPE_DIM], k_pe], axis=-1)
    o_c = blocked_causal_softmax(q, k, kv[..., QK_NOPE_DIM:], (QK_NOPE_DIM + QK_ROPE_DIM) ** -0.5)
    dil = lambda t: t.reshape(b, s, N_HEADS_DIL, HEAD_DIM)
    o_d = dilated_attention(dil(qd), dil(kd), dil(vd))
    o = jnp.concatenate([o_c.reshape(b, s, N_HEADS_MLA * V_HEAD_DIM), o_d.reshape(b, s, DIL_W)], axis=-1)
    return o @ w_o


def swiglu(h, w_gate, w_up, w_down):
    return (jax.nn.silu(h @ w_gate) * (h @ w_up)) @ w_down


def moe_swiglu(h, w_router, w_gate, w_up, w_down):
    b, s, d = h.shape
    t = h.reshape(b * s, d)
    logits = (t @ w_router).astype(jnp.float32)
    top_val, top_idx = lax.top_k(logits, TOP_K)
    gates = jax.nn.softmax(top_val, axis=-1)
    combine = jnp.sum(jax.nn.one_hot(top_idx, N_EXPERTS, dtype=jnp.float32) * gates[..., None], axis=1)
    out = jnp.zeros_like(t)
    for e in range(N_EXPERTS):
        out = out + combine[:, e:e + 1].astype(t.dtype) * swiglu(t, w_gate[e], w_up[e], w_down[e])
    return out.reshape(b, s, d)


def setup_inputs(seed: int = 0) -> dict:
    key = jax.random.key(seed)
    keys = iter(jax.random.split(key, 24))
    ne, no = (DEPTH + 1) // 2, DEPTH // 2
    f32 = jnp.float32

    def dense(shape, fan_in):
        return jax.random.normal(next(keys), shape, f32) * (fan_in ** -0.5)

    def gain(shape):
        return 1.0 + 0.02 * jax.random.normal(next(keys), shape, f32)

    return {
        'x': jax.random.normal(next(keys), (BATCH, SEQ, D_MODEL), f32),
        'even_norm_mix': gain((ne, D_MODEL)),
        'even_w_in': dense((ne, D_MODEL, EVEN_IN), D_MODEL),
        'even_b_f': jax.random.uniform(next(keys), (ne, N_HEADS_FOX), f32, 1.0, 4.0),
        'even_w_o': dense((ne, EVEN_OUT, D_MODEL), EVEN_OUT),
        'even_norm_ffn': gain((ne, D_MODEL)),
        'even_w_gate': dense((ne, D_MODEL, D_FF_DENSE), D_MODEL),
        'even_w_up': dense((ne, D_MODEL, D_FF_DENSE), D_MODEL),
        'even_w_down': dense((ne, D_FF_DENSE, D_MODEL), D_FF_DENSE),
        'odd_norm_mix': gain((no, D_MODEL)),
        'odd_w_in': dense((no, D_MODEL, ODD_IN), D_MODEL),
        'odd_g_cq': gain((no, Q_LORA_RANK)),
        'odd_g_ckv': gain((no, KV_LORA_RANK)),
        'odd_w_uq': dense((no, Q_LORA_RANK, N_HEADS_MLA * (QK_NOPE_DIM + QK_ROPE_DIM)), Q_LORA_RANK),
        'odd_w_ukv': dense((no, KV_LORA_RANK, N_HEADS_MLA * (QK_NOPE_DIM + V_HEAD_DIM)), KV_LORA_RANK),
        'odd_w_o': dense((no, ODD_OUT, D_MODEL), ODD_OUT),
        'odd_norm_ffn': gain((no, D_MODEL)),
        'odd_w_router': dense((no, D_MODEL, N_EXPERTS), D_MODEL),
        'odd_w_exp_gate': dense((no, N_EXPERTS, D_MODEL, D_FF_EXPERT), D_MODEL),
        'odd_w_exp_up': dense((no, N_EXPERTS, D_MODEL, D_FF_EXPERT), D_MODEL),
        'odd_w_exp_down': dense((no, N_EXPERTS, D_FF_EXPERT, D_MODEL), D_FF_EXPERT),
        'final_norm': gain((D_MODEL,)),
    }


def reference(x, even_norm_mix, even_w_in, even_b_f, even_w_o, even_norm_ffn, even_w_gate, even_w_up,
              even_w_down, odd_norm_mix, odd_w_in, odd_g_cq, odd_g_ckv, odd_w_uq, odd_w_ukv, odd_w_o,
              odd_norm_ffn, odd_w_router, odd_w_exp_gate, odd_w_exp_up, odd_w_exp_down, final_norm):
    pos = jnp.arange(x.shape[1])
    h = x
    for layer in range(DEPTH):
        i = layer // 2
        if layer % 2 == 0:
            h = h + even_mixer(rmsnorm(h, even_norm_mix[i]), even_w_in[i], even_b_f[i], even_w_o[i])
            h = h + swiglu(rmsnorm(h, even_norm_ffn[i]), even_w_gate[i], even_w_up[i], even_w_down[i])
        else:
            h = h + odd_mixer(rmsnorm(h, odd_norm_mix[i]), odd_w_in[i], odd_g_cq[i], odd_g_ckv[i],
                              odd_w_uq[i], odd_w_ukv[i], odd_w_o[i], pos)
            h = h + moe_swiglu(rmsnorm(h, odd_norm_ffn[i]), odd_w_router[i], odd_w_exp_gate[i],
                               odd_w_exp_up[i], odd_w_exp_down[i])
    return rmsnorm(h, final_norm)
```

```python
import functools

import jax
import jax.numpy as jnp
from jax import lax
from jax.experimental import pallas as pl
from jax.experimental.pallas import tpu as pltpu

F32 = jnp.float32
BF16 = jnp.bfloat16

LANE = 128
SUBLANE = 8
HEAD_DIM = 128
N_HEADS = 8
Q_LORA = 512
KV_LORA = 512
ROPE_DIM = 64
MLA_QK = 256
MLA_SCALE_DIM = 192
ROPE_THETA = 10000.0
DIL_RATES = (1, 4, 16)
DIL_BLOCK = 128
N_EXPERTS = 8
NORM_EPS = 1e-6
NEG_INF = float("-inf")

ROW_TILE = 512
MM_TILE = 1024
ATT_TQ = 512
ATT_TK = 512
SB_TK = 256
FFN_TM = 512
FFN_TF = 512
DMA_ROWS = 256


def _rms(x, g):
    ms = jnp.mean(x * x, axis=-1, keepdims=True)
    return x * lax.rsqrt(ms + NORM_EPS) * g


def _log_sigmoid(x):
    return -(jnp.maximum(-x, 0.0) + jnp.log1p(jnp.exp(-jnp.abs(x))))


def _rmsnorm_body(x_ref, g_ref, o_ref):
    o_ref[...] = _rms(x_ref[...].astype(F32), g_ref[...]).astype(o_ref.dtype)


def rmsnorm(x, g, out_dtype, *, col_block=0, width=None):
    m = x.shape[0]
    width = x.shape[1] if width is None else width
    tm = min(ROW_TILE, m)
    return pl.pallas_call(
        _rmsnorm_body,
        grid=(m // tm,),
        in_specs=[pl.BlockSpec((tm, width), lambda i: (i, col_block)),
                  pl.BlockSpec((1, width), lambda i: (0, 0))],
        out_specs=pl.BlockSpec((tm, width), lambda i: (i, 0)),
        out_shape=jax.ShapeDtypeStruct((m, width), out_dtype),
        name="rmsnorm",
    )(x, g.reshape(1, width).astype(F32))


def _matmul_body(*refs, n_in, has_res):
    o_ref = refs[-1]
    acc = jnp.dot(refs[0][...], refs[n_in][...], preferred_element_type=F32)
    for i in range(1, n_in):
        acc = acc + jnp.dot(refs[i][...], refs[n_in + i][...], preferred_element_type=F32)
    if has_res:
        acc = refs[2 * n_in][...] + acc
    o_ref[...] = acc.astype(o_ref.dtype)


def matmul(a_list, w_list, out_dtype, res=None):
    m = a_list[0].shape[0]
    n = w_list[0].shape[1]
    tm = min(MM_TILE, m)
    tn = MM_TILE if n % MM_TILE == 0 else n
    in_specs = [pl.BlockSpec((tm, a.shape[1]), lambda i, j: (i, 0)) for a in a_list]
    in_specs += [pl.BlockSpec((w.shape[0], tn), lambda i, j: (0, j)) for w in w_list]
    args = list(a_list) + list(w_list)
    if res is not None:
        in_specs.append(pl.BlockSpec((tm, tn), lambda i, j: (i, j)))
        args.append(res)
    return pl.pallas_call(
        functools.partial(_matmul_body, n_in=len(a_list), has_res=res is not None),
        grid=(m // tm, n // tn),
        in_specs=in_specs,
        out_specs=pl.BlockSpec((tm, tn), lambda i, j: (i, j)),
        out_shape=jax.ShapeDtypeStruct((m, n), out_dtype),
        name="matmul",
    )(*args)


def _decay_body(fg_ref, b_ref, c_ref, *, seq):
    bias = b_ref[...]
    row = lax.broadcasted_iota(jnp.int32, (SUBLANE, LANE), 0)

    def body(i, carry):
        st = pl.multiple_of(i * SUBLANE, SUBLANE)
        x = _log_sigmoid(fg_ref[pl.ds(st, SUBLANE), :] + bias)
        for s in (1, 2, 4):
            x = x + jnp.where(row >= s, pltpu.roll(x, s, 0), 0.0)
        x = x + carry
        c_ref[pl.ds(st, SUBLANE), :] = x
        return jnp.broadcast_to(x[SUBLANE - 1:SUBLANE, :], (SUBLANE, LANE))

    lax.fori_loop(0, seq // SUBLANE, body, jnp.zeros((SUBLANE, LANE), F32), unroll=8)


def decay_cumsum(fg, b_f, batch, seq):
    bias = jnp.zeros((1, LANE), F32).at[0, :b_f.shape[0]].set(b_f.astype(F32))
    return pl.pallas_call(
        functools.partial(_decay_body, seq=seq),
        grid=(batch,),
        in_specs=[pl.BlockSpec((seq, LANE), lambda b: (b, 0)),
                  pl.BlockSpec((1, LANE), lambda b: (0, 0))],
        out_specs=pl.BlockSpec((seq, LANE), lambda b: (b, 0)),
        out_shape=jax.ShapeDtypeStruct((batch * seq, LANE), F32),
        name="decay_cumsum",
    )(fg, bias)


def _flash_body(*refs, tq, tk, scale, has_decay):
    if has_decay:
        q_ref, k_ref, v_ref, ccol_ref, crow_ref, o_ref = refs
    else:
        q_ref, k_ref, v_ref, o_ref = refs
    h = pl.program_id(1)
    qi = pl.program_id(2)
    q = q_ref[...]
    if has_decay:
        cblk = ccol_ref[...]
        lane = lax.broadcasted_iota(jnp.int32, cblk.shape, 1)
        cq = jnp.sum(jnp.where(lane == h, cblk, 0.0), axis=-1, keepdims=True)

    def step(kb, carry, masked):
        m, l, acc = carry
        start = pl.multiple_of(kb * tk, tk)
        k = k_ref[pl.ds(start, tk), :]
        v = v_ref[pl.ds(start, tk), :]
        s = lax.dot_general(q, k, (((1,), (1,)), ((), ())), preferred_element_type=F32) * scale
        if has_decay:
            s = s + (cq - crow_ref[:, pl.ds(start, tk)])
        if masked:
            row = lax.broadcasted_iota(jnp.int32, (tq, tk), 0)
            col = lax.broadcasted_iota(jnp.int32, (tq, tk), 1)
            s = jnp.where(col <= row, s, NEG_INF)
        m_new = jnp.maximum(m, jnp.max(s, axis=-1, keepdims=True))
        alpha = jnp.exp(m - m_new)
        p = jnp.exp(s - m_new)
        l = alpha * l + jnp.sum(p, axis=-1, keepdims=True)
        acc = alpha * acc + jnp.dot(p.astype(BF16), v, preferred_element_type=F32)
        return m_new, l, acc

    init = (jnp.full((tq, 1), NEG_INF, F32), jnp.zeros((tq, 1), F32),
            jnp.zeros((tq, v_ref.shape[1]), F32))
    carry = lax.fori_loop(0, qi, lambda kb, c: step(kb, c, False), init)
    m, l, acc = step(qi, carry, True)
    o_ref[...] = (acc / l).astype(o_ref.dtype)


def flash_attention(q_arr, k_arr, v_arr, *, batch, seq, dqk, q_col, k_col, v_col, scale,
                    decay=None):
    tq = tk = min(ATT_TQ, seq)
    nq = seq // tq
    in_specs = [pl.BlockSpec((tq, dqk), lambda b, h, i: (b * nq + i, q_col(h))),
                pl.BlockSpec((seq, dqk), lambda b, h, i: (b, k_col(h))),
                pl.BlockSpec((seq, HEAD_DIM), lambda b, h, i: (b, v_col(h)))]
    args = [q_arr, k_arr, v_arr]
    if decay is not None:
        in_specs += [pl.BlockSpec((tq, LANE), lambda b, h, i: (b * nq + i, 0)),
                     pl.BlockSpec((None, None, 1, seq), lambda b, h, i: (b, h, 0, 0))]
        args += list(decay)
    return pl.pallas_call(
        functools.partial(_flash_body, tq=tq, tk=tk, scale=scale, has_decay=decay is not None),
        grid=(batch, N_HEADS, nq),
        in_specs=in_specs,
        out_specs=pl.BlockSpec((tq, HEAD_DIM), lambda b, h, i: (b * nq + i, h)),
        out_shape=jax.ShapeDtypeStruct((batch * seq, N_HEADS * HEAD_DIM), BF16),
        name="flash_attention",
    )(*args)


def _sb_body(q_ref, k_ref, v_ref, o_ref, *, tq, tk, scale):
    qi = pl.program_id(2)
    q = q_ref[...]
    jrow = lax.broadcasted_iota(jnp.int32, (tk, tk), 0)
    scol = lax.broadcasted_iota(jnp.int32, (tk, tk), 1)
    later = jnp.where(jrow > scol, 1.0, 0.0).astype(BF16)

    def block(start, carry, mask):
        run, acc = carry
        k = k_ref[pl.ds(start, tk), :]
        v = v_ref[pl.ds(start, tk), :]
        z = lax.dot_general(q, k, (((1,), (1,)), ((), ())), preferred_element_type=F32) * scale
        sp = jnp.maximum(z, 0.0) + jnp.log1p(jnp.exp(-jnp.abs(z)))
        l1m = -sp
        if mask is not None:
            l1m = jnp.where(mask, l1m, 0.0)
        hi = l1m.astype(BF16)
        lo = (l1m - hi.astype(F32)).astype(BF16)
        after = (jnp.dot(hi, later, preferred_element_type=F32)
                 + jnp.dot(lo, later, preferred_element_type=F32))
        a = jnp.exp((z - sp) + after + run)
        if mask is not None:
            a = jnp.where(mask, a, 0.0)
        acc = acc + jnp.dot(a.astype(BF16), v, preferred_element_type=F32)
        run = run + jnp.sum(l1m, axis=-1, keepdims=True)
        return run, acc

    carry = (jnp.zeros((tq, 1), F32), jnp.zeros((tq, HEAD_DIM), F32))
    ri = lax.broadcasted_iota(jnp.int32, (tq, tk), 0)
    ci = lax.broadcasted_iota(jnp.int32, (tq, tk), 1)
    for mb in reversed(range(tq // tk)):
        start = pl.multiple_of(qi * tq + mb * tk, tk)
        carry = block(start, carry, (ci + mb * tk) < ri)
    n_full = qi * (tq // tk)

    def full(i, c):
        return block(pl.multiple_of((n_full - 1 - i) * tk, tk), c, None)

    _, acc = lax.fori_loop(0, n_full, full, carry)
    o_ref[...] = acc.astype(o_ref.dtype)


def stick_breaking_attention(qkv, *, batch, seq, q_col, k_col, v_col):
    tq = min(ATT_TQ, seq)
    tk = min(SB_TK, tq)
    nq = seq // tq
    return pl.pallas_call(
        functools.partial(_sb_body, tq=tq, tk=tk, scale=HEAD_DIM ** -0.5),
        grid=(batch, N_HEADS, nq),
        in_specs=[pl.BlockSpec((tq, HEAD_DIM), lambda b, h, i: (b * nq + i, q_col(h))),
                  pl.BlockSpec((seq, HEAD_DIM), lambda b, h, i: (b, k_col(h))),
                  pl.BlockSpec((seq, HEAD_DIM), lambda b, h, i: (b, v_col(h)))],
        out_specs=pl.BlockSpec((tq, HEAD_DIM), lambda b, h, i: (b * nq + i, h)),
        out_shape=jax.ShapeDtypeStruct((batch * seq, N_HEADS * HEAD_DIM), BF16),
        name="stick_breaking",
    )(qkv, qkv, qkv)


def _mla_prep_body(q_ref, kr_ref, kv_ref, cos_ref, sin_lo_ref, sin_hi_ref, qf_ref, kf_ref):
    cos, sin_lo, sin_hi = cos_ref[...], sin_lo_ref[...], sin_hi_ref[...]

    def rope(x):
        return x * cos + pltpu.roll(x, LANE - ROPE_DIM // 2, 1) * sin_lo \
            + pltpu.roll(x, ROPE_DIM // 2, 1) * sin_hi

    k_pe = rope(kr_ref[...]).astype(BF16)
    for h in range(N_HEADS):
        lo = h * MLA_QK
        qf_ref[:, lo:lo + HEAD_DIM] = q_ref[:, lo:lo + HEAD_DIM].astype(BF16)
        qf_ref[:, lo + HEAD_DIM:lo + MLA_QK] = rope(q_ref[:, lo + HEAD_DIM:lo + MLA_QK]).astype(BF16)
        kf_ref[:, lo:lo + HEAD_DIM] = kv_ref[:, lo:lo + HEAD_DIM]
        kf_ref[:, lo + HEAD_DIM:lo + MLA_QK] = k_pe


def mla_prep(q_raw, lat, kv, tables, *, seq):
    m = q_raw.shape[0]
    tm = min(ROW_TILE, seq)
    npos = seq // tm
    wide = N_HEADS * MLA_QK
    kr_block = (Q_LORA + KV_LORA) // LANE
    tab_spec = pl.BlockSpec((tm, LANE), lambda i: (i % npos, 0))
    return pl.pallas_call(
        _mla_prep_body,
        grid=(m // tm,),
        in_specs=[pl.BlockSpec((tm, wide), lambda i: (i, 0)),
                  pl.BlockSpec((tm, LANE), lambda i: (i, kr_block)),
                  pl.BlockSpec((tm, wide), lambda i: (i, 0)),
                  tab_spec, tab_spec, tab_spec],
        out_specs=[pl.BlockSpec((tm, wide), lambda i: (i, 0)),
                   pl.BlockSpec((tm, wide), lambda i: (i, 0))],
        out_shape=[jax.ShapeDtypeStruct((m, wide), BF16),
                   jax.ShapeDtypeStruct((m, wide), BF16)],
        name="mla_prep",
    )(q_raw, lat, kv, *tables)


def rope_tables(seq):
    half = ROPE_DIM // 2
    inv_freq = ROPE_THETA ** (-jnp.arange(half, dtype=F32) / half)
    ang = jnp.arange(seq, dtype=F32)[:, None] * inv_freq[None, :]
    cos, sin = jnp.cos(ang), jnp.sin(ang)
    zeros = jnp.zeros((seq, half), F32)
    pad = jnp.zeros((seq, LANE - ROPE_DIM), F32)
    cos_t = jnp.concatenate([cos, cos, pad], axis=1)
    sin_lo = jnp.concatenate([-sin, zeros, pad], axis=1)
    sin_hi = jnp.concatenate([zeros, sin, pad], axis=1)
    return cos_t, sin_lo, sin_hi


def _dilated_body(q_ref, kp_ref, kc_ref, vp_ref, vc_ref, o_ref, lse_ref, *, rate, scale):
    ib = pl.program_id(2)
    n = DIL_BLOCK
    qi = lax.broadcasted_iota(jnp.int32, (n, 2 * n), 0)
    ki = lax.broadcasted_iota(jnp.int32, (n, 2 * n), 1)
    steps = n + qi - ki
    valid = (steps >= 0) & (steps <= n) & ((ki >= n) | (ib > 0))
    dist = (steps * rate).astype(F32)
    lane = lax.broadcasted_iota(jnp.int32, (n, LANE), 1)
    lse_all = jnp.zeros((n, LANE), F32)
    for h in range(N_HEADS):
        cols = slice(h * HEAD_DIM, (h + 1) * HEAD_DIM)
        q = q_ref[:, cols]
        k = jnp.concatenate([kp_ref[:, cols], kc_ref[:, cols]], axis=0)
        v = jnp.concatenate([vp_ref[:, cols], vc_ref[:, cols]], axis=0)
        s = lax.dot_general(q, k, (((1,), (1,)), ((), ())), preferred_element_type=F32) * scale
        s = jnp.where(valid, s - (2.0 ** -(h + 1)) * dist, NEG_INF)
        m = jnp.max(s, axis=-1, keepdims=True)
        p = jnp.exp(s - m)
        l = jnp.sum(p, axis=-1, keepdims=True)
        o = jnp.dot(p.astype(BF16), v, preferred_element_type=F32) / l
        o_ref[:, cols] = o.astype(o_ref.dtype)
        lse_all = jnp.where(lane == h, m + jnp.log(l), lse_all)
    lse_ref[...] = lse_all


def dilated_branch(dqkv, *, batch, seq, rate):
    wide = N_HEADS * HEAD_DIM
    n_rows = batch * seq // rate
    nb = seq // rate // DIL_BLOCK
    view = dqkv.reshape(n_rows, rate * 3 * wide)

    def spec(which, prev):
        def index(b, c, i):
            blk = jnp.maximum(i - 1, 0) if prev else i
            return (b * nb + blk, c * 3 + which)
        return pl.BlockSpec((DIL_BLOCK, wide), index)

    o, lse = pl.pallas_call(
        functools.partial(_dilated_body, rate=rate, scale=HEAD_DIM ** -0.5),
        grid=(batch, rate, nb),
        in_specs=[spec(0, False), spec(1, True), spec(1, False), spec(2, True), spec(2, False)],
        out_specs=[pl.BlockSpec((DIL_BLOCK, wide), lambda b, c, i: (b * nb + i, c)),
                   pl.BlockSpec((DIL_BLOCK, LANE), lambda b, c, i: (b * nb + i, c))],
        out_shape=[jax.ShapeDtypeStruct((n_rows, rate * wide), BF16),
                   jax.ShapeDtypeStruct((n_rows, rate * LANE), F32)],
        name="dilated_branch",
    )(view, view, view, view, view)
    return o.reshape(batch * seq, wide), lse.reshape(batch * seq, LANE)


def _dilated_merge_body(*refs):
    n = len(DIL_RATES)
    o_refs, l_refs, out_ref = refs[:n], refs[n:2 * n], refs[2 * n]
    lses = [r[...] for r in l_refs]
    top = functools.reduce(jnp.maximum, lses)
    es = [jnp.exp(x - top) for x in lses]
    den = functools.reduce(lambda a, b: a + b, es)
    ws = [e / den for e in es]
    for h in range(N_HEADS):
        cols = slice(h * HEAD_DIM, (h + 1) * HEAD_DIM)
        acc = ws[0][:, h:h + 1] * o_refs[0][:, cols].astype(F32)
        for g in range(1, n):
            acc = acc + ws[g][:, h:h + 1] * o_refs[g][:, cols].astype(F32)
        out_ref[:, cols] = acc.astype(out_ref.dtype)


def dilated_merge(outs, lses):
    m, wide = outs[0].shape
    tm = min(ROW_TILE, m)
    return pl.pallas_call(
        _dilated_merge_body,
        grid=(m // tm,),
        in_specs=[pl.BlockSpec((tm, wide), lambda i: (i, 0)) for _ in outs]
        + [pl.BlockSpec((tm, LANE), lambda i: (i, 0)) for _ in lses],
        out_specs=pl.BlockSpec((tm, wide), lambda i: (i, 0)),
        out_shape=jax.ShapeDtypeStruct((m, wide), BF16),
        name="dilated_merge",
    )(*outs, *lses)


def _ffn_body(*refs, grouped, n_tiles):
    if grouped:
        meta_ref, x_ref, wg_ref, wu_ref, wd_ref, o_ref, xb_ref = refs
    else:
        x_ref, wg_ref, wu_ref, wd_ref, res_ref, o_ref = refs
    f = pl.program_id(1)

    def compute():
        if grouped:
            @pl.when(f == 0)
            def _():
                xb_ref[...] = x_ref[...].astype(BF16)
            x = xb_ref[...]
        else:
            x = x_ref[...]
        g = jnp.dot(x, wg_ref[...], preferred_element_type=F32)
        u = jnp.dot(x, wu_ref[...], preferred_element_type=F32)
        mid = (g * (1.0 / (1.0 + jnp.exp(-g))) * u).astype(BF16)
        part = jnp.dot(mid, wd_ref[...], preferred_element_type=F32)

        @pl.when(f == 0)
        def _():
            o_ref[...] = part if grouped else res_ref[...] + part

        @pl.when(f > 0)
        def _():
            o_ref[...] += part

    if grouped:
        in_use = pl.program_id(0) < meta_ref[n_tiles]
        pl.when(in_use)(compute)

        @pl.when(jnp.logical_not(in_use) & (f == 0))
        def _():
            o_ref[...] = jnp.zeros_like(o_ref)
    else:
        compute()


def dense_ffn(x, w_gate, w_up, w_down, res):
    m, d = x.shape
    ff = w_gate.shape[1]
    tm, tf = min(FFN_TM, m), min(FFN_TF, ff)
    return pl.pallas_call(
        functools.partial(_ffn_body, grouped=False, n_tiles=0),
        grid=(m // tm, ff // tf),
        in_specs=[pl.BlockSpec((tm, d), lambda i, f: (i, 0)),
                  pl.BlockSpec((d, tf), lambda i, f: (0, f)),
                  pl.BlockSpec((d, tf), lambda i, f: (0, f)),
                  pl.BlockSpec((tf, d), lambda i, f: (f, 0)),
                  pl.BlockSpec((tm, d), lambda i, f: (i, 0))],
        out_specs=pl.BlockSpec((tm, d), lambda i, f: (i, 0)),
        out_shape=jax.ShapeDtypeStruct((m, d), F32),
        name="dense_ffn",
    )(x, w_gate, w_up, w_down, res)


def grouped_ffn(xs, meta, w_gate, w_up, w_down, *, tm):
    ns, d = xs.shape
    n_tiles = ns // tm
    ff = w_gate.shape[2]
    tf = min(FFN_TF, ff)
    nf = ff // tf

    def tile(i, meta):
        return jnp.minimum(i, meta[n_tiles] - 1)

    def chunk(i, f, meta):
        return jnp.where(i < meta[n_tiles], f, nf - 1)

    grid_spec = pltpu.PrefetchScalarGridSpec(
        num_scalar_prefetch=1,
        grid=(n_tiles, nf),
        in_specs=[pl.BlockSpec((tm, d), lambda i, f, meta: (tile(i, meta), 0)),
                  pl.BlockSpec((None, d, tf), lambda i, f, meta: (meta[tile(i, meta)], 0, chunk(i, f, meta))),
                  pl.BlockSpec((None, d, tf), lambda i, f, meta: (meta[tile(i, meta)], 0, chunk(i, f, meta))),
                  pl.BlockSpec((None, tf, d), lambda i, f, meta: (meta[tile(i, meta)], chunk(i, f, meta), 0))],
        out_specs=pl.BlockSpec((tm, d), lambda i, f, meta: (i, 0)),
        scratch_shapes=[pltpu.VMEM((tm, d), BF16)],
    )
    return pl.pallas_call(
        functools.partial(_ffn_body, grouped=True, n_tiles=n_tiles),
        grid_spec=grid_spec,
        out_shape=jax.ShapeDtypeStruct((ns, d), F32),
        name="grouped_ffn",
    )(meta, xs, w_gate, w_up, w_down)


ROUTE_I1, ROUTE_I2, ROUTE_G1, ROUTE_G2, ROUTE_P1, ROUTE_P2 = range(6)


def _router_body(h_ref, g_ref, wr_ref, route_ref, cnt_ref, seen_ref, *, tr):
    @pl.when(pl.program_id(0) == 0)
    def _():
        seen_ref[...] = jnp.zeros_like(seen_ref)

    y = _rms(h_ref[...], g_ref[...])
    logits = jnp.dot(y, wr_ref[...], preferred_element_type=F32, precision=lax.Precision.HIGHEST)
    lane = lax.broadcasted_iota(jnp.int32, (tr, LANE), 1)
    l_a = jnp.where(lane < N_EXPERTS, logits, NEG_INF)
    m1 = jnp.max(l_a, axis=-1, keepdims=True)
    i1 = jnp.min(jnp.where(l_a == m1, lane, LANE), axis=-1, keepdims=True)
    l_b = jnp.where(lane == i1, NEG_INF, l_a)
    m2 = jnp.max(l_b, axis=-1, keepdims=True)
    i2 = jnp.min(jnp.where(l_b == m2, lane, LANE), axis=-1, keepdims=True)
    e = jnp.exp(m2 - m1)
    g1 = 1.0 / (1.0 + e)
    g2 = e / (1.0 + e)
    sel = jnp.where((lane == i1) | (lane == i2), 1.0, 0.0)
    r = lax.broadcasted_iota(jnp.int32, (tr, tr), 0)
    c = lax.broadcasted_iota(jnp.int32, (tr, tr), 1)
    earlier = jnp.where(r > c, 1.0, 0.0).astype(BF16)
    rank = jnp.dot(earlier, sel.astype(BF16), preferred_element_type=F32) + seen_ref[0:1, :]
    p1 = jnp.sum(jnp.where(lane == i1, rank, 0.0), axis=-1, keepdims=True)
    p2 = jnp.sum(jnp.where(lane == i2, rank, 0.0), axis=-1, keepdims=True)
    seen = seen_ref[...] + jnp.sum(sel, axis=0, keepdims=True)
    seen_ref[...] = seen
    cnt_ref[...] = seen
    route = jnp.zeros((tr, LANE), F32)
    for idx, val in ((ROUTE_I1, i1.astype(F32)), (ROUTE_I2, i2.astype(F32)), (ROUTE_G1, g1),
                     (ROUTE_G2, g2), (ROUTE_P1, p1), (ROUTE_P2, p2)):
        route = jnp.where(lane == idx, val, route)
    route_ref[...] = route


def router(h, g, w_router):
    m, d = h.shape
    tr = min(ROW_TILE, m)
    wr = jnp.zeros((d, LANE), F32).at[:, :N_EXPERTS].set(w_router.astype(F32))
    return pl.pallas_call(
        functools.partial(_router_body, tr=tr),
        grid=(m // tr,),
        in_specs=[pl.BlockSpec((tr, d), lambda i: (i, 0)),
                  pl.BlockSpec((1, d), lambda i: (0, 0)),
                  pl.BlockSpec((d, LANE), lambda i: (0, 0))],
        out_specs=[pl.BlockSpec((tr, LANE), lambda i: (i, 0)),
                   pl.BlockSpec((SUBLANE, LANE), lambda i: (0, 0))],
        out_shape=[jax.ShapeDtypeStruct((m, LANE), F32),
                   jax.ShapeDtypeStruct((SUBLANE, LANE), F32)],
        scratch_shapes=[pltpu.VMEM((SUBLANE, LANE), F32)],
        name="router",
    )(h, g.reshape(1, d).astype(F32), wr)


def _row_copy(src, dst, sem):
    return pltpu.make_async_copy(src, dst, sem)


def _scatter_body(s1_ref, s2_ref, h_ref, g_ref, xs_in_ref, xs_ref, buf_ref, sem, *, tr):
    del xs_in_ref
    base = pl.program_id(0) * tr
    buf_ref[...] = _rms(h_ref[...], g_ref[...])

    def copies(r):
        src = buf_ref.at[pl.ds(r, 1)]
        return (_row_copy(src, xs_ref.at[pl.ds(s1_ref[base + r], 1)], sem),
                _row_copy(src, xs_ref.at[pl.ds(s2_ref[base + r], 1)], sem))

    def start(r, _):
        for cp in copies(r):
            cp.start()
        return 0

    def wait(r, _):
        for cp in copies(r):
            cp.wait()
        return 0

    lax.fori_loop(0, tr, start, 0)
    lax.fori_loop(0, tr, wait, 0)


def scatter_rows(h, g, slot1, slot2, n_slots):
    m, d = h.shape
    tr = min(DMA_ROWS, m)
    grid_spec = pltpu.PrefetchScalarGridSpec(
        num_scalar_prefetch=2,
        grid=(m // tr,),
        in_specs=[pl.BlockSpec((tr, d), lambda i, s1, s2: (i, 0)),
                  pl.BlockSpec((1, d), lambda i, s1, s2: (0, 0)),
                  pl.BlockSpec(memory_space=pl.ANY)],
        out_specs=pl.BlockSpec(memory_space=pl.ANY),
        scratch_shapes=[pltpu.VMEM((tr, d), F32), pltpu.SemaphoreType.DMA(())],
    )
    return pl.pallas_call(
        functools.partial(_scatter_body, tr=tr),
        grid_spec=grid_spec,
        out_shape=jax.ShapeDtypeStruct((n_slots, d), F32),
        input_output_aliases={4: 0},
        name="scatter_rows",
    )(slot1, slot2, h, g.reshape(1, d).astype(F32), jnp.zeros((n_slots, d), F32))


def _combine_body(s1_ref, s2_ref, h_ref, route_ref, y_ref, o_ref, buf_ref, sem, *, tr):
    base = pl.program_id(0) * tr

    def copies(r):
        return (_row_copy(y_ref.at[pl.ds(s1_ref[base + r], 1)], buf_ref.at[0, pl.ds(r, 1)], sem),
                _row_copy(y_ref.at[pl.ds(s2_ref[base + r], 1)], buf_ref.at[1, pl.ds(r, 1)], sem))

    def start(r, _):
        for cp in copies(r):
            cp.start()
        return 0

    def wait(r, _):
        for cp in copies(r):
            cp.wait()
        return 0

    lax.fori_loop(0, tr, start, 0)
    lax.fori_loop(0, tr, wait, 0)
    route = route_ref[...]
    g1 = route[:, ROUTE_G1:ROUTE_G1 + 1]
    g2 = route[:, ROUTE_G2:ROUTE_G2 + 1]
    o_ref[...] = h_ref[...] + (g1 * buf_ref[0] + g2 * buf_ref[1])


def combine_rows(h, route, y, slot1, slot2):
    m, d = h.shape
    tr = min(DMA_ROWS, m)
    grid_spec = pltpu.PrefetchScalarGridSpec(
        num_scalar_prefetch=2,
        grid=(m // tr,),
        in_specs=[pl.BlockSpec((tr, d), lambda i, s1, s2: (i, 0)),
                  pl.BlockSpec((tr, LANE), lambda i, s1, s2: (i, 0)),
                  pl.BlockSpec(memory_space=pl.ANY)],
        out_specs=pl.BlockSpec((tr, d), lambda i, s1, s2: (i, 0)),
        scratch_shapes=[pltpu.VMEM((2, tr, d), F32), pltpu.SemaphoreType.DMA(())],
    )
    return pl.pallas_call(
        functools.partial(_combine_body, tr=tr),
        grid_spec=grid_spec,
        out_shape=jax.ShapeDtypeStruct((m, d), F32),
        name="combine_rows",
    )(slot1, slot2, h, route, y)


def moe_ffn(h, g, w_router, w_gate, w_up, w_down):
    m, _ = h.shape
    tm = min(FFN_TM, m)
    n_tiles = (2 * m) // tm + N_EXPERTS
    route, counts = router(h, g, w_router)
    col = lambda j: route[:, j].astype(jnp.int32)
    cnt = counts[0, :N_EXPERTS].astype(jnp.int32)
    tiles_per = (cnt + tm - 1) // tm
    tile_end = jnp.cumsum(tiles_per)
    group_start = (tile_end - tiles_per) * tm
    slot1 = group_start[col(ROUTE_I1)] + col(ROUTE_P1)
    slot2 = group_start[col(ROUTE_I2)] + col(ROUTE_P2)
    tile_expert = jnp.sum(jnp.arange(n_tiles)[:, None] >= tile_end[None, :], axis=1)
    meta = jnp.concatenate([jnp.minimum(tile_expert, N_EXPERTS - 1), tile_end[-1:]]).astype(jnp.int32)
    xs = scatter_rows(h, g, slot1, slot2, n_tiles * tm)
    ys = grouped_ffn(xs, meta, w_gate, w_up, w_down, tm=tm)
    return combine_rows(h, route, ys, slot1, slot2)


def even_layer(h, p, *, batch, seq):
    wide = N_HEADS * HEAD_DIM
    w_in = p["w_in"]
    w_qkv = w_in[:, :6 * wide].astype(BF16)
    w_fg = jnp.zeros((w_in.shape[0], LANE), BF16).at[:, :N_HEADS].set(w_in[:, 6 * wide:].astype(BF16))
    hn = rmsnorm(h, p["norm_mix"], BF16)
    qkv = matmul([hn], [w_qkv], BF16)
    fg = matmul([hn], [w_fg], F32)
    c = decay_cumsum(fg, p["b_f"], batch, seq)
    c_row = c[:, :N_HEADS].reshape(batch, seq, N_HEADS).transpose(0, 2, 1).reshape(batch, N_HEADS, 1, seq)
    o_a = flash_attention(qkv, qkv, qkv, batch=batch, seq=seq, dqk=HEAD_DIM,
                          q_col=lambda hd: hd, k_col=lambda hd: N_HEADS + hd,
                          v_col=lambda hd: 2 * N_HEADS + hd, scale=HEAD_DIM ** -0.5,
                          decay=(c, c_row))
    o_b = stick_breaking_attention(qkv, batch=batch, seq=seq, q_col=lambda hd: 3 * N_HEADS + hd,
                                   k_col=lambda hd: 4 * N_HEADS + hd,
                                   v_col=lambda hd: 5 * N_HEADS + hd)
    w_o = p["w_o"].astype(BF16)
    h = matmul([o_a, o_b], [w_o[:wide], w_o[wide:]], F32, res=h)
    hn = rmsnorm(h, p["norm_ffn"], BF16)
    return dense_ffn(hn, p["w_gate"].astype(BF16), p["w_up"].astype(BF16),
                     p["w_down"].astype(BF16), h)


def odd_layer(h, p, tables, *, batch, seq):
    wide = N_HEADS * HEAD_DIM
    d = h.shape[1]
    w_in = p["w_in"]
    lat_w = Q_LORA + KV_LORA
    w_lat = jnp.zeros((d, lat_w + LANE), BF16).at[:, :lat_w + ROPE_DIM].set(
        w_in[:, :lat_w + ROPE_DIM].astype(BF16))
    w_dil = w_in[:, lat_w + ROPE_DIM:].astype(BF16)
    w_uq = jnp.zeros((Q_LORA, N_HEADS, MLA_QK), BF16).at[:, :, :HEAD_DIM + ROPE_DIM].set(
        p["w_uq"].reshape(Q_LORA, N_HEADS, HEAD_DIM + ROPE_DIM).astype(BF16)).reshape(Q_LORA, N_HEADS * MLA_QK)
    hn = rmsnorm(h, p["norm_mix"], BF16)
    lat = matmul([hn], [w_lat], F32)
    dqkv = matmul([hn], [w_dil], BF16)
    cqn = rmsnorm(lat, p["g_cq"], BF16, col_block=0, width=Q_LORA)
    ckvn = rmsnorm(lat, p["g_ckv"], BF16, col_block=1, width=KV_LORA)
    q_raw = matmul([cqn], [w_uq], F32)
    kv = matmul([ckvn], [p["w_ukv"].astype(BF16)], BF16)
    q_full, k_full = mla_prep(q_raw, lat, kv, tables, seq=seq)
    o_c = flash_attention(q_full, k_full, kv, batch=batch, seq=seq, dqk=MLA_QK,
                          q_col=lambda hd: hd, k_col=lambda hd: hd, v_col=lambda hd: 2 * hd + 1,
                          scale=MLA_SCALE_DIM ** -0.5)
    branches = [dilated_branch(dqkv, batch=batch, seq=seq, rate=r) for r in DIL_RATES]
    o_d = dilated_merge([o for o, _ in branches], [l for _, l in branches])
    w_o = p["w_o"].astype(BF16)
    h = matmul([o_c, o_d], [w_o[:wide], w_o[wide:]], F32, res=h)
    return moe_ffn(h, p["norm_ffn"], p["w_router"], p["w_exp_gate"].astype(BF16),
                   p["w_exp_up"].astype(BF16), p["w_exp_down"].astype(BF16))


def kernel(x, even_norm_mix, even_w_in, even_b_f, even_w_o, even_norm_ffn, even_w_gate, even_w_up,
           even_w_down, odd_norm_mix, odd_w_in, odd_g_cq, odd_g_ckv, odd_w_uq, odd_w_ukv, odd_w_o,
           odd_norm_ffn, odd_w_router, odd_w_exp_gate, odd_w_exp_up, odd_w_exp_down, final_norm):
    batch, seq, d = x.shape
    assert seq % (max(DIL_RATES) * DIL_BLOCK) == 0
    even = dict(norm_mix=even_norm_mix, w_in=even_w_in, b_f=even_b_f, w_o=even_w_o,
                norm_ffn=even_norm_ffn, w_gate=even_w_gate, w_up=even_w_up, w_down=even_w_down)
    odd = dict(norm_mix=odd_norm_mix, w_in=odd_w_in, g_cq=odd_g_cq, g_ckv=odd_g_ckv, w_uq=odd_w_uq,
               w_ukv=odd_w_ukv, w_o=odd_w_o, norm_ffn=odd_norm_ffn, w_router=odd_w_router,
               w_exp_gate=odd_w_exp_gate, w_exp_up=odd_w_exp_up, w_exp_down=odd_w_exp_down)
    depth = even_w_in.shape[0] + odd_w_in.shape[0]
    tables = rope_tables(seq)
    h = x.reshape(batch * seq, d)
    for layer in range(depth):
        i = layer // 2
        if layer % 2 == 0:
            h = even_layer(h, {k: v[i] for k, v in even.items()}, batch=batch, seq=seq)
        else:
            h = odd_layer(h, {k: v[i] for k, v in odd.items()}, tables, batch=batch, seq=seq)
    return rmsnorm(h, final_norm, x.dtype).reshape(batch, seq, d)
```

```python
import functools

import jax
import jax.numpy as jnp
from jax import lax
from jax.experimental import pallas as pl
from jax.experimental.pallas import tpu as pltpu

F32 = jnp.float32
BF16 = jnp.bfloat16

LANE = 128
SUBLANE = 8
HEAD_DIM = 128
N_HEADS = 8
Q_LORA = 512
KV_LORA = 512
ROPE_DIM = 64
MLA_QK = 256
MLA_SCALE_DIM = 192
ROPE_THETA = 10000.0
DIL_RATES = (1, 4, 16)
DIL_BLOCK = 128
N_EXPERTS = 8
NORM_EPS = 1e-6
NEG_INF = float("-inf")
LOG2E = 1.4426950408889634

ROW_TILE = 512
MM_TILE = 1024
ATT_TQ = 512
ATT_TK = 512
SB_TK = 256
FFN_TM = 512
FFN_TF = 512
MOE_TM = 1024
MOE_SUB = 512
MOE_TF = 256
DMA_ROWS = 256


def _rms(x, g):
    ms = jnp.mean(x * x, axis=-1, keepdims=True)
    return x * lax.rsqrt(ms + NORM_EPS) * g


def _log_sigmoid(x):
    return -(jnp.maximum(-x, 0.0) + jnp.log1p(jnp.exp(-jnp.abs(x))))


def _rmsnorm_body(x_ref, g_ref, o_ref):
    o_ref[...] = _rms(x_ref[...].astype(F32), g_ref[...]).astype(o_ref.dtype)


def rmsnorm(x, g, out_dtype, *, col_block=0, width=None):
    m = x.shape[0]
    width = x.shape[1] if width is None else width
    tm = min(ROW_TILE, m)
    return pl.pallas_call(
        _rmsnorm_body,
        grid=(m // tm,),
        in_specs=[pl.BlockSpec((tm, width), lambda i: (i, col_block)),
                  pl.BlockSpec((1, width), lambda i: (0, 0))],
        out_specs=pl.BlockSpec((tm, width), lambda i: (i, 0)),
        out_shape=jax.ShapeDtypeStruct((m, width), out_dtype),
        name="rmsnorm",
    )(x, g.reshape(1, width).astype(F32))


def _matmul_body(*refs, n_in, has_res):
    o_ref = refs[-1]
    acc = jnp.dot(refs[0][...], refs[n_in][...], preferred_element_type=F32)
    for i in range(1, n_in):
        acc = acc + jnp.dot(refs[i][...], refs[n_in + i][...], preferred_element_type=F32)
    if has_res:
        acc = refs[2 * n_in][...] + acc
    o_ref[...] = acc.astype(o_ref.dtype)


def matmul(a_list, w_list, out_dtype, res=None, *, layer=0, n=None):
    m = a_list[0].shape[0]
    n = w_list[0][0].shape[2] if n is None else n
    tm = min(MM_TILE, m)
    tn = MM_TILE if n % MM_TILE == 0 else n

    def w_spec(a, row_block):
        return pl.BlockSpec((None, a.shape[1], tn), lambda i, j: (layer, row_block, j))

    in_specs = [pl.BlockSpec((tm, a.shape[1]), lambda i, j: (i, 0)) for a in a_list]
    in_specs += [w_spec(a, rb) for a, (_, rb) in zip(a_list, w_list)]
    args = list(a_list) + [w for w, _ in w_list]
    if res is not None:
        in_specs.append(pl.BlockSpec((tm, tn), lambda i, j: (i, j)))
        args.append(res)
    return pl.pallas_call(
        functools.partial(_matmul_body, n_in=len(a_list), has_res=res is not None),
        grid=(m // tm, n // tn),
        in_specs=in_specs,
        out_specs=pl.BlockSpec((tm, tn), lambda i, j: (i, j)),
        out_shape=jax.ShapeDtypeStruct((m, n), out_dtype),
        name="matmul",
    )(*args)


def _decay_body(fg_ref, b_ref, c_ref, *, seq):
    bias = b_ref[...]
    row = lax.broadcasted_iota(jnp.int32, (SUBLANE, LANE), 0)

    def body(i, carry):
        st = pl.multiple_of(i * SUBLANE, SUBLANE)
        x = _log_sigmoid(fg_ref[pl.ds(st, SUBLANE), :] + bias)
        for s in (1, 2, 4):
            x = x + jnp.where(row >= s, pltpu.roll(x, s, 0), 0.0)
        x = x + carry
        c_ref[pl.ds(st, SUBLANE), :] = x
        return jnp.broadcast_to(x[SUBLANE - 1:SUBLANE, :], (SUBLANE, LANE))

    lax.fori_loop(0, seq // SUBLANE, body, jnp.zeros((SUBLANE, LANE), F32), unroll=8)


def decay_cumsum(fg, b_f, batch, seq):
    bias = jnp.zeros((1, LANE), F32).at[0, :b_f.shape[0]].set(b_f.astype(F32))
    return pl.pallas_call(
        functools.partial(_decay_body, seq=seq),
        grid=(batch,),
        in_specs=[pl.BlockSpec((seq, LANE), lambda b: (b, 0)),
                  pl.BlockSpec((1, LANE), lambda b: (0, 0))],
        out_specs=pl.BlockSpec((seq, LANE), lambda b: (b, 0)),
        out_shape=jax.ShapeDtypeStruct((batch * seq, LANE), F32),
        name="decay_cumsum",
    )(fg, bias)


def _flash_body(*refs, tq, tk, scale, has_decay):
    if has_decay:
        q_ref, k_ref, v_ref, ccol_ref, crow_ref, o_ref = refs
    else:
        q_ref, k_ref, v_ref, o_ref = refs
    h = pl.program_id(1)
    qi = pl.program_id(2)
    q = q_ref[...]
    to_log2 = scale * LOG2E
    if has_decay:
        cblk = ccol_ref[...]
        lane = lax.broadcasted_iota(jnp.int32, cblk.shape, 1)
        cq = jnp.sum(jnp.where(lane == h, cblk, 0.0), axis=-1, keepdims=True) * LOG2E

    def span(kb, n, carry, masked):
        m, l, acc = carry
        start = pl.multiple_of(kb * tk, tk)
        k = k_ref[pl.ds(start, n * tk), :]
        t_all = lax.dot_general(q, k, (((1,), (1,)), ((), ())), preferred_element_type=F32) * to_log2
        if has_decay:
            t_all = t_all - crow_ref[:, pl.ds(start, n * tk)] * LOG2E
        for j in range(n):
            t = t_all[:, j * tk:(j + 1) * tk]
            if masked:
                row = lax.broadcasted_iota(jnp.int32, (tq, tk), 0)
                col = lax.broadcasted_iota(jnp.int32, (tq, tk), 1)
                t = jnp.where(col <= row, t, NEG_INF)
            row_max = jnp.max(t, axis=-1, keepdims=True)
            if has_decay:
                row_max = row_max + cq
            m_new = jnp.maximum(m, row_max)
            shift = m_new - cq if has_decay else m_new
            alpha = jnp.exp2(m - m_new)
            p = jnp.exp2(t - shift)
            l = alpha * l + jnp.sum(p, axis=-1, keepdims=True)
            v = v_ref[pl.ds(pl.multiple_of(start + j * tk, tk), tk), :]
            acc = alpha * acc + jnp.dot(p.astype(BF16), v, preferred_element_type=F32)
            m = m_new
        return m, l, acc

    init = (jnp.full((tq, 1), NEG_INF, F32), jnp.zeros((tq, 1), F32),
            jnp.zeros((tq, v_ref.shape[1]), F32))
    carry = lax.fori_loop(0, qi // 2, lambda i, c: span(2 * i, 2, c, False), init)
    carry = lax.cond(qi % 2 == 1, lambda c: span(qi - 1, 1, c, False), lambda c: c, carry)
    m, l, acc = span(qi, 1, carry, True)
    o_ref[...] = (acc / l).astype(o_ref.dtype)


def flash_attention(q_arr, k_arr, v_arr, *, batch, seq, dqk, q_col, k_col, v_col, scale,
                    decay=None):
    tq = tk = min(ATT_TQ, seq)
    nq = seq // tq
    in_specs = [pl.BlockSpec((tq, dqk), lambda b, h, i: (b * nq + i, q_col(h))),
                pl.BlockSpec((seq, dqk), lambda b, h, i: (b, k_col(h))),
                pl.BlockSpec((seq, HEAD_DIM), lambda b, h, i: (b, v_col(h)))]
    args = [q_arr, k_arr, v_arr]
    if decay is not None:
        in_specs += [pl.BlockSpec((tq, LANE), lambda b, h, i: (b * nq + i, 0)),
                     pl.BlockSpec((None, None, 1, seq), lambda b, h, i: (b, h, 0, 0))]
        args += list(decay)
    return pl.pallas_call(
        functools.partial(_flash_body, tq=tq, tk=tk, scale=scale, has_decay=decay is not None),
        grid=(batch, N_HEADS, nq),
        in_specs=in_specs,
        out_specs=pl.BlockSpec((tq, HEAD_DIM), lambda b, h, i: (b * nq + i, h)),
        out_shape=jax.ShapeDtypeStruct((batch * seq, N_HEADS * HEAD_DIM), BF16),
        name="flash_attention",
    )(*args)


def _sb_body(q_ref, k_ref, v_ref, o_ref, *, tq, tk, scale):
    qi = pl.program_id(2)
    q = q_ref[...]
    jrow = lax.broadcasted_iota(jnp.int32, (tk, tk), 0)
    scol = lax.broadcasted_iota(jnp.int32, (tk, tk), 1)
    later = jnp.where(jrow > scol, 1.0, 0.0).astype(BF16)
    later2 = jnp.concatenate([later, later], axis=0)

    def span(q_block, n_q, carry, masked):
        run, acc = carry
        width = n_q * tq
        base = pl.multiple_of(q_block * tq, tq)
        k = k_ref[pl.ds(base, width), :]
        v = v_ref[pl.ds(base, width), :]
        z = lax.dot_general(q, k, (((1,), (1,)), ((), ())), preferred_element_type=F32) * scale
        sp = jnp.maximum(z, 0.0) + jnp.log(1.0 + jnp.exp(-jnp.abs(z)))
        l1m = -sp
        if masked:
            strict = (lax.broadcasted_iota(jnp.int32, (tq, width), 1)
                      < lax.broadcasted_iota(jnp.int32, (tq, width), 0))
            l1m = jnp.where(strict, l1m, 0.0)
        hi = l1m.astype(BF16)
        lo = (l1m - hi.astype(F32)).astype(BF16)
        blocks = [slice(j * tk, (j + 1) * tk) for j in range(width // tk)]
        after = [jnp.dot(jnp.concatenate([hi[:, b], lo[:, b]], axis=1), later2,
                         preferred_element_type=F32) for b in blocks]
        sums = [jnp.sum(l1m[:, b], axis=-1, keepdims=True) for b in blocks]
        logw = []
        for j in reversed(range(len(blocks))):
            logw.append((z[:, blocks[j]] - sp[:, blocks[j]]) + after[j] + run)
            run = run + sums[j]
        a = jnp.exp(jnp.concatenate(logw[::-1], axis=1))
        if masked:
            a = jnp.where(strict, a, 0.0)
        acc = acc + jnp.dot(a.astype(BF16), v, preferred_element_type=F32)
        return run, acc

    carry = (jnp.zeros((tq, 1), F32), jnp.zeros((tq, HEAD_DIM), F32))
    carry = span(qi, 1, carry, True)
    carry = lax.cond(qi % 2 == 1, lambda c: span(qi - 1, 1, c, False), lambda c: c, carry)
    n_pairs = qi // 2
    _, acc = lax.fori_loop(0, n_pairs, lambda i, c: span(2 * (n_pairs - 1 - i), 2, c, False), carry)
    o_ref[...] = acc.astype(o_ref.dtype)


def stick_breaking_attention(qkv, *, batch, seq, q_col, k_col, v_col):
    tq = min(ATT_TQ, seq)
    tk = min(SB_TK, tq)
    nq = seq // tq
    return pl.pallas_call(
        functools.partial(_sb_body, tq=tq, tk=tk, scale=HEAD_DIM ** -0.5),
        grid=(batch, N_HEADS, nq),
        in_specs=[pl.BlockSpec((tq, HEAD_DIM), lambda b, h, i: (b * nq + i, q_col(h))),
                  pl.BlockSpec((seq, HEAD_DIM), lambda b, h, i: (b, k_col(h))),
                  pl.BlockSpec((seq, HEAD_DIM), lambda b, h, i: (b, v_col(h)))],
        out_specs=pl.BlockSpec((tq, HEAD_DIM), lambda b, h, i: (b * nq + i, h)),
        out_shape=jax.ShapeDtypeStruct((batch * seq, N_HEADS * HEAD_DIM), BF16),
        name="stick_breaking",
    )(qkv, qkv, qkv)


def _mla_prep_body(q_ref, kr_ref, kv_ref, cos_ref, sin_lo_ref, sin_hi_ref, qf_ref, kf_ref):
    cos, sin_lo, sin_hi = cos_ref[...], sin_lo_ref[...], sin_hi_ref[...]

    def rope(x):
        return x * cos + pltpu.roll(x, LANE - ROPE_DIM // 2, 1) * sin_lo \
            + pltpu.roll(x, ROPE_DIM // 2, 1) * sin_hi

    k_pe = rope(kr_ref[...]).astype(BF16)
    for h in range(N_HEADS):
        lo = h * MLA_QK
        qf_ref[:, lo:lo + HEAD_DIM] = q_ref[:, lo:lo + HEAD_DIM].astype(BF16)
        qf_ref[:, lo + HEAD_DIM:lo + MLA_QK] = rope(q_ref[:, lo + HEAD_DIM:lo + MLA_QK]).astype(BF16)
        kf_ref[:, lo:lo + HEAD_DIM] = kv_ref[:, lo:lo + HEAD_DIM]
        kf_ref[:, lo + HEAD_DIM:lo + MLA_QK] = k_pe


def mla_prep(q_raw, lat, kv, tables, *, seq):
    m = q_raw.shape[0]
    tm = min(ROW_TILE, seq)
    npos = seq // tm
    wide = N_HEADS * MLA_QK
    kr_block = (Q_LORA + KV_LORA) // LANE
    tab_spec = pl.BlockSpec((tm, LANE), lambda i: (i % npos, 0))
    return pl.pallas_call(
        _mla_prep_body,
        grid=(m // tm,),
        in_specs=[pl.BlockSpec((tm, wide), lambda i: (i, 0)),
                  pl.BlockSpec((tm, LANE), lambda i: (i, kr_block)),
                  pl.BlockSpec((tm, wide), lambda i: (i, 0)),
                  tab_spec, tab_spec, tab_spec],
        out_specs=[pl.BlockSpec((tm, wide), lambda i: (i, 0)),
                   pl.BlockSpec((tm, wide), lambda i: (i, 0))],
        out_shape=[jax.ShapeDtypeStruct((m, wide), BF16),
                   jax.ShapeDtypeStruct((m, wide), BF16)],
        name="mla_prep",
    )(q_raw, lat, kv, *tables)


def rope_tables(seq):
    half = ROPE_DIM // 2
    inv_freq = ROPE_THETA ** (-jnp.arange(half, dtype=F32) / half)
    ang = jnp.arange(seq, dtype=F32)[:, None] * inv_freq[None, :]
    cos, sin = jnp.cos(ang), jnp.sin(ang)
    zeros = jnp.zeros((seq, half), F32)
    pad = jnp.zeros((seq, LANE - ROPE_DIM), F32)
    cos_t = jnp.concatenate([cos, cos, pad], axis=1)
    sin_lo = jnp.concatenate([-sin, zeros, pad], axis=1)
    sin_hi = jnp.concatenate([zeros, sin, pad], axis=1)
    return cos_t, sin_lo, sin_hi


def _dilated_body(q_ref, kp_ref, kc_ref, vp_ref, vc_ref, o_ref, lse_ref, *, rate, scale):
    ib = pl.program_id(2)
    n = DIL_BLOCK
    qi = lax.broadcasted_iota(jnp.int32, (n, 2 * n), 0)
    ki = lax.broadcasted_iota(jnp.int32, (n, 2 * n), 1)
    steps = n + qi - ki
    valid = (steps >= 0) & (steps <= n) & ((ki >= n) | (ib > 0))
    dist = (steps * rate).astype(F32)
    lane = lax.broadcasted_iota(jnp.int32, (n, LANE), 1)
    lse_all = jnp.zeros((n, LANE), F32)
    for h in range(N_HEADS):
        cols = slice(h * HEAD_DIM, (h + 1) * HEAD_DIM)
        q = q_ref[:, cols]
        k = jnp.concatenate([kp_ref[:, cols], kc_ref[:, cols]], axis=0)
        v = jnp.concatenate([vp_ref[:, cols], vc_ref[:, cols]], axis=0)
        s = lax.dot_general(q, k, (((1,), (1,)), ((), ())), preferred_element_type=F32) * scale
        s = jnp.where(valid, s - (2.0 ** -(h + 1)) * dist, NEG_INF)
        m = jnp.max(s, axis=-1, keepdims=True)
        p = jnp.exp(s - m)
        l = jnp.sum(p, axis=-1, keepdims=True)
        o = jnp.dot(p.astype(BF16), v, preferred_element_type=F32) / l
        o_ref[:, cols] = o.astype(o_ref.dtype)
        lse_all = jnp.where(lane == h, m + jnp.log(l), lse_all)
    lse_ref[...] = lse_all


def dilated_branch(dqkv, *, batch, seq, rate):
    wide = N_HEADS * HEAD_DIM
    n_rows = batch * seq // rate
    nb = seq // rate // DIL_BLOCK
    view = dqkv.reshape(n_rows, rate * 3 * wide)

    def spec(which, prev):
        def index(b, c, i):
            blk = jnp.maximum(i - 1, 0) if prev else i
            return (b * nb + blk, c * 3 + which)
        return pl.BlockSpec((DIL_BLOCK, wide), index)

    o, lse = pl.pallas_call(
        functools.partial(_dilated_body, rate=rate, scale=HEAD_DIM ** -0.5),
        grid=(batch, rate, nb),
        in_specs=[spec(0, False), spec(1, True), spec(1, False), spec(2, True), spec(2, False)],
        out_specs=[pl.BlockSpec((DIL_BLOCK, wide), lambda b, c, i: (b * nb + i, c)),
                   pl.BlockSpec((DIL_BLOCK, LANE), lambda b, c, i: (b * nb + i, c))],
        out_shape=[jax.ShapeDtypeStruct((n_rows, rate * wide), BF16),
                   jax.ShapeDtypeStruct((n_rows, rate * LANE), F32)],
        name="dilated_branch",
    )(view, view, view, view, view)
    return o.reshape(batch * seq, wide), lse.reshape(batch * seq, LANE)


def _dilated_merge_body(*refs):
    n = len(DIL_RATES)
    o_refs, l_refs, out_ref = refs[:n], refs[n:2 * n], refs[2 * n]
    lses = [r[...] for r in l_refs]
    top = functools.reduce(jnp.maximum, lses)
    es = [jnp.exp(x - top) for x in lses]
    den = functools.reduce(lambda a, b: a + b, es)
    ws = [e / den for e in es]
    for h in range(N_HEADS):
        cols = slice(h * HEAD_DIM, (h + 1) * HEAD_DIM)
        acc = ws[0][:, h:h + 1] * o_refs[0][:, cols].astype(F32)
        for g in range(1, n):
            acc = acc + ws[g][:, h:h + 1] * o_refs[g][:, cols].astype(F32)
        out_ref[:, cols] = acc.astype(out_ref.dtype)


def dilated_merge(outs, lses):
    m, wide = outs[0].shape
    tm = min(ROW_TILE, m)
    return pl.pallas_call(
        _dilated_merge_body,
        grid=(m // tm,),
        in_specs=[pl.BlockSpec((tm, wide), lambda i: (i, 0)) for _ in outs]
        + [pl.BlockSpec((tm, LANE), lambda i: (i, 0)) for _ in lses],
        out_specs=pl.BlockSpec((tm, wide), lambda i: (i, 0)),
        out_shape=jax.ShapeDtypeStruct((m, wide), BF16),
        name="dilated_merge",
    )(*outs, *lses)


def _swiglu_partial(x, wg, wu, wd):
    g = jnp.dot(x, wg, preferred_element_type=F32)
    u = jnp.dot(x, wu, preferred_element_type=F32)
    mid = (g * (1.0 / (1.0 + jnp.exp(-g))) * u).astype(BF16)
    return jnp.dot(mid, wd, preferred_element_type=F32)


def _dense_ffn_body(x_ref, wg_ref, wu_ref, wd_ref, res_ref, o_ref):
    @pl.when(pl.program_id(1) == 0)
    def _():
        o_ref[...] = res_ref[...]

    o_ref[...] += _swiglu_partial(x_ref[...], wg_ref[...], wu_ref[...], wd_ref[...])


def dense_ffn(x, w_gate, w_up, w_down, res, *, layer):
    m, d = x.shape
    ff = w_gate.shape[2]
    tm, tf = min(FFN_TM, m), min(FFN_TF, ff)
    return pl.pallas_call(
        _dense_ffn_body,
        grid=(m // tm, ff // tf),
        in_specs=[pl.BlockSpec((tm, d), lambda i, f: (i, 0)),
                  pl.BlockSpec((None, d, tf), lambda i, f: (layer, 0, f)),
                  pl.BlockSpec((None, d, tf), lambda i, f: (layer, 0, f)),
                  pl.BlockSpec((None, tf, d), lambda i, f: (layer, f, 0)),
                  pl.BlockSpec((tm, d), lambda i, f: (i, 0))],
        out_specs=pl.BlockSpec((tm, d), lambda i, f: (i, 0)),
        out_shape=jax.ShapeDtypeStruct((m, d), F32),
        name="dense_ffn",
    )(x, w_gate, w_up, w_down, res)


def _pack_bf16_pairs(lo, hi):
    lo_bits = lax.bitcast_convert_type(lo.astype(BF16).astype(F32), jnp.uint32)
    hi_bits = lax.bitcast_convert_type(hi.astype(BF16).astype(F32), jnp.uint32)
    return (lo_bits >> 16) | (hi_bits & jnp.uint32(0xFFFF0000))


def _unpack_bf16_pairs(packed):
    lo = lax.bitcast_convert_type(packed << 16, F32).astype(BF16)
    hi = lax.bitcast_convert_type(packed & jnp.uint32(0xFFFF0000), F32).astype(BF16)
    return lo, hi


def _grouped_ffn_body(meta_ref, x_ref, wg_ref, wu_ref, wd_ref, o_ref, xb_ref, *, n_tiles, sub):
    i = pl.program_id(0)
    f = pl.program_id(1)
    in_use = i < meta_ref[n_tiles]
    rows = meta_ref[n_tiles + 1 + i]
    half = x_ref.shape[1]

    @pl.when(f == 0)
    def _():
        o_ref[...] = jnp.zeros_like(o_ref)

    @pl.when(in_use & (f == 0))
    def _():
        lo, hi = _unpack_bf16_pairs(x_ref[...])
        xb_ref[:, :half] = lo
        xb_ref[:, half:] = hi

    for s in range(x_ref.shape[0] // sub):
        @pl.when(in_use & (rows > s * sub))
        def _():
            sl = pl.ds(s * sub, sub)
            o_ref[sl, :] += _swiglu_partial(xb_ref[sl, :], wg_ref[...].astype(BF16),
                                            wu_ref[...].astype(BF16), wd_ref[...].astype(BF16))


def grouped_ffn(xs, meta, w_gate, w_up, w_down, *, layer, tm):
    ns, half = xs.shape
    d = 2 * half
    n_tiles = ns // tm
    ff = w_gate.shape[3]
    tf = min(MOE_TF, ff)
    nf = ff // tf

    def tile(i, meta):
        return jnp.minimum(i, meta[n_tiles] - 1)

    def expert(i, meta):
        return meta[tile(i, meta)]

    def chunk(i, f, meta):
        return jnp.where(i < meta[n_tiles], f, nf - 1)

    grid_spec = pltpu.PrefetchScalarGridSpec(
        num_scalar_prefetch=1,
        grid=(n_tiles, nf),
        in_specs=[pl.BlockSpec((tm, half), lambda i, f, meta: (tile(i, meta), 0)),
                  pl.BlockSpec((None, None, d, tf),
                               lambda i, f, meta: (layer, expert(i, meta), 0, chunk(i, f, meta))),
                  pl.BlockSpec((None, None, d, tf),
                               lambda i, f, meta: (layer, expert(i, meta), 0, chunk(i, f, meta))),
                  pl.BlockSpec((None, None, tf, d),
                               lambda i, f, meta: (layer, expert(i, meta), chunk(i, f, meta), 0))],
        out_specs=pl.BlockSpec((tm, d), lambda i, f, meta: (i, 0)),
        scratch_shapes=[pltpu.VMEM((tm, d), BF16)],
    )
    return pl.pallas_call(
        functools.partial(_grouped_ffn_body, n_tiles=n_tiles, sub=min(MOE_SUB, tm)),
        grid_spec=grid_spec,
        out_shape=jax.ShapeDtypeStruct((ns, d), F32),
        name="grouped_ffn",
    )(meta, xs, w_gate, w_up, w_down)


ROUTE_I1, ROUTE_I2, ROUTE_G1, ROUTE_G2, ROUTE_P1, ROUTE_P2 = range(6)


def _router_body(h_ref, g_ref, wr_ref, route_ref, cnt_ref, seen_ref, *, tr):
    @pl.when(pl.program_id(0) == 0)
    def _():
        seen_ref[...] = jnp.zeros_like(seen_ref)

    y = _rms(h_ref[...], g_ref[...])
    logits = jnp.dot(y, wr_ref[...], preferred_element_type=F32, precision=lax.Precision.HIGHEST)
    lane = lax.broadcasted_iota(jnp.int32, (tr, LANE), 1)
    l_a = jnp.where(lane < N_EXPERTS, logits, NEG_INF)
    m1 = jnp.max(l_a, axis=-1, keepdims=True)
    i1 = jnp.min(jnp.where(l_a == m1, lane, LANE), axis=-1, keepdims=True)
    l_b = jnp.where(lane == i1, NEG_INF, l_a)
    m2 = jnp.max(l_b, axis=-1, keepdims=True)
    i2 = jnp.min(jnp.where(l_b == m2, lane, LANE), axis=-1, keepdims=True)
    e = jnp.exp(m2 - m1)
    g1 = 1.0 / (1.0 + e)
    g2 = e / (1.0 + e)
    sel = jnp.where((lane == i1) | (lane == i2), 1.0, 0.0)
    r = lax.broadcasted_iota(jnp.int32, (tr, tr), 0)
    c = lax.broadcasted_iota(jnp.int32, (tr, tr), 1)
    earlier = jnp.where(r > c, 1.0, 0.0).astype(BF16)
    rank = jnp.dot(earlier, sel.astype(BF16), preferred_element_type=F32) + seen_ref[0:1, :]
    p1 = jnp.sum(jnp.where(lane == i1, rank, 0.0), axis=-1, keepdims=True)
    p2 = jnp.sum(jnp.where(lane == i2, rank, 0.0), axis=-1, keepdims=True)
    seen = seen_ref[...] + jnp.sum(sel, axis=0, keepdims=True)
    seen_ref[...] = seen
    cnt_ref[...] = seen
    route = jnp.zeros((tr, LANE), F32)
    for idx, val in ((ROUTE_I1, i1.astype(F32)), (ROUTE_I2, i2.astype(F32)), (ROUTE_G1, g1),
                     (ROUTE_G2, g2), (ROUTE_P1, p1), (ROUTE_P2, p2)):
        route = jnp.where(lane == idx, val, route)
    route_ref[...] = route


def router(h, g, w_router):
    m, d = h.shape
    tr = min(ROW_TILE, m)
    wr = jnp.zeros((d, LANE), F32).at[:, :N_EXPERTS].set(w_router.astype(F32))
    return pl.pallas_call(
        functools.partial(_router_body, tr=tr),
        grid=(m // tr,),
        in_specs=[pl.BlockSpec((tr, d), lambda i: (i, 0)),
                  pl.BlockSpec((1, d), lambda i: (0, 0)),
                  pl.BlockSpec((d, LANE), lambda i: (0, 0))],
        out_specs=[pl.BlockSpec((tr, LANE), lambda i: (i, 0)),
                   pl.BlockSpec((SUBLANE, LANE), lambda i: (0, 0))],
        out_shape=[jax.ShapeDtypeStruct((m, LANE), F32),
                   jax.ShapeDtypeStruct((SUBLANE, LANE), F32)],
        scratch_shapes=[pltpu.VMEM((SUBLANE, LANE), F32)],
        name="router",
    )(h, g.reshape(1, d).astype(F32), wr)


def _row_copy(src, dst, sem):
    return pltpu.make_async_copy(src, dst, sem)


def _scatter_body(s1_ref, s2_ref, h_ref, g_ref, xs_in_ref, xs_ref, buf_ref, sem, *, tr):
    del xs_in_ref
    base = pl.program_id(0) * tr
    y = _rms(h_ref[...], g_ref[...])
    half = y.shape[1] // 2
    buf_ref[...] = _pack_bf16_pairs(y[:, :half], y[:, half:])

    def copies(r):
        src = buf_ref.at[pl.ds(r, 1)]
        return (_row_copy(src, xs_ref.at[pl.ds(s1_ref[base + r], 1)], sem),
                _row_copy(src, xs_ref.at[pl.ds(s2_ref[base + r], 1)], sem))

    def start(r, _):
        for queue, cp in enumerate(copies(r)):
            cp.start(priority=queue)
        return 0

    def wait(r, _):
        for cp in copies(r):
            cp.wait()
        return 0

    lax.fori_loop(0, tr, start, 0)
    lax.fori_loop(0, tr, wait, 0)


def scatter_rows(h, g, slot1, slot2, n_slots):
    m, d = h.shape
    tr = min(DMA_ROWS, m)
    grid_spec = pltpu.PrefetchScalarGridSpec(
        num_scalar_prefetch=2,
        grid=(m // tr,),
        in_specs=[pl.BlockSpec((tr, d), lambda i, s1, s2: (i, 0)),
                  pl.BlockSpec((1, d), lambda i, s1, s2: (0, 0)),
                  pl.BlockSpec(memory_space=pl.ANY)],
        out_specs=pl.BlockSpec(memory_space=pl.ANY),
        scratch_shapes=[pltpu.VMEM((tr, d // 2), jnp.uint32), pltpu.SemaphoreType.DMA(())],
    )
    return pl.pallas_call(
        functools.partial(_scatter_body, tr=tr),
        grid_spec=grid_spec,
        out_shape=jax.ShapeDtypeStruct((n_slots, d // 2), jnp.uint32),
        input_output_aliases={4: 0},
        name="scatter_rows",
    )(slot1, slot2, h, g.reshape(1, d).astype(F32), jnp.zeros((n_slots, d // 2), jnp.uint32))


def _combine_body(s1_ref, s2_ref, h_ref, route_ref, y_ref, o_ref, buf_ref, sem, *, tr):
    base = pl.program_id(0) * tr

    def copies(r):
        return (_row_copy(y_ref.at[pl.ds(s1_ref[base + r], 1)], buf_ref.at[0, pl.ds(r, 1)], sem),
                _row_copy(y_ref.at[pl.ds(s2_ref[base + r], 1)], buf_ref.at[1, pl.ds(r, 1)], sem))

    def start(r, _):
        for queue, cp in enumerate(copies(r)):
            cp.start(priority=queue)
        return 0

    def wait(r, _):
        for cp in copies(r):
            cp.wait()
        return 0

    lax.fori_loop(0, tr, start, 0)
    lax.fori_loop(0, tr, wait, 0)
    route = route_ref[...]
    g1 = route[:, ROUTE_G1:ROUTE_G1 + 1]
    g2 = route[:, ROUTE_G2:ROUTE_G2 + 1]
    o_ref[...] = h_ref[...] + (g1 * buf_ref[0] + g2 * buf_ref[1])


def combine_rows(h, route, y, slot1, slot2):
    m, d = h.shape
    tr = min(DMA_ROWS, m)
    grid_spec = pltpu.PrefetchScalarGridSpec(
        num_scalar_prefetch=2,
        grid=(m // tr,),
        in_specs=[pl.BlockSpec((tr, d), lambda i, s1, s2: (i, 0)),
                  pl.BlockSpec((tr, LANE), lambda i, s1, s2: (i, 0)),
                  pl.BlockSpec(memory_space=pl.ANY)],
        out_specs=pl.BlockSpec((tr, d), lambda i, s1, s2: (i, 0)),
        scratch_shapes=[pltpu.VMEM((2, tr, d), F32), pltpu.SemaphoreType.DMA(())],
    )
    return pl.pallas_call(
        functools.partial(_combine_body, tr=tr),
        grid_spec=grid_spec,
        out_shape=jax.ShapeDtypeStruct((m, d), F32),
        name="combine_rows",
    )(slot1, slot2, h, route, y)


def moe_ffn(h, g, w_router, w_gate, w_up, w_down, *, layer):
    m, _ = h.shape
    tm = min(MOE_TM, m)
    n_tiles = -(-(2 * m) // tm) + N_EXPERTS
    route, counts = router(h, g, w_router)
    col = lambda j: route[:, j].astype(jnp.int32)
    cnt = counts[0, :N_EXPERTS].astype(jnp.int32)
    tiles_per = (cnt + tm - 1) // tm
    tile_end = jnp.cumsum(tiles_per)
    tile_start = tile_end - tiles_per
    group_start = tile_start * tm
    slot1 = group_start[col(ROUTE_I1)] + col(ROUTE_P1)
    slot2 = group_start[col(ROUTE_I2)] + col(ROUTE_P2)
    tile_ids = jnp.arange(n_tiles)
    tile_expert = jnp.minimum(jnp.sum(tile_ids[:, None] >= tile_end[None, :], axis=1), N_EXPERTS - 1)
    tile_rows = jnp.clip(cnt[tile_expert] - (tile_ids - tile_start[tile_expert]) * tm, 0, tm)
    meta = jnp.concatenate([tile_expert, tile_end[-1:], tile_rows]).astype(jnp.int32)
    xs = scatter_rows(h, g, slot1, slot2, n_tiles * tm)
    ys = grouped_ffn(xs, meta, w_gate, w_up, w_down, layer=layer, tm=tm)
    return combine_rows(h, route, ys, slot1, slot2)


def prep_even(p):
    wide = N_HEADS * HEAD_DIM
    w_in = p["w_in"]
    n_layers, d, _ = w_in.shape
    w_fg = jnp.zeros((n_layers, d, LANE), BF16).at[:, :, :N_HEADS].set(w_in[:, :, 6 * wide:].astype(BF16))
    return dict(p, w_qkv=w_in[:, :, :6 * wide].astype(BF16), w_fg=w_fg, w_o=p["w_o"].astype(BF16),
                w_gate=p["w_gate"].astype(BF16), w_up=p["w_up"].astype(BF16),
                w_down=p["w_down"].astype(BF16))


def even_layer(h, p, i, *, batch, seq):
    hn = rmsnorm(h, p["norm_mix"][i], BF16)
    qkv = matmul([hn], [(p["w_qkv"], 0)], BF16, layer=i)
    fg = matmul([hn], [(p["w_fg"], 0)], F32, layer=i)
    c = decay_cumsum(fg, p["b_f"][i], batch, seq)
    c_row = c[:, :N_HEADS].reshape(batch, seq, N_HEADS).transpose(0, 2, 1).reshape(batch, N_HEADS, 1, seq)
    o_a = flash_attention(qkv, qkv, qkv, batch=batch, seq=seq, dqk=HEAD_DIM,
                          q_col=lambda hd: hd, k_col=lambda hd: N_HEADS + hd,
                          v_col=lambda hd: 2 * N_HEADS + hd, scale=HEAD_DIM ** -0.5,
                          decay=(c, c_row))
    o_b = stick_breaking_attention(qkv, batch=batch, seq=seq, q_col=lambda hd: 3 * N_HEADS + hd,
                                   k_col=lambda hd: 4 * N_HEADS + hd,
                                   v_col=lambda hd: 5 * N_HEADS + hd)
    h = matmul([o_a, o_b], [(p["w_o"], 0), (p["w_o"], 1)], F32, res=h, layer=i)
    hn = rmsnorm(h, p["norm_ffn"][i], BF16)
    return dense_ffn(hn, p["w_gate"], p["w_up"], p["w_down"], h, layer=i)


def prep_odd(p):
    w_in = p["w_in"]
    n_layers, d, _ = w_in.shape
    lat_w = Q_LORA + KV_LORA
    w_lat = jnp.zeros((n_layers, d, lat_w + LANE), BF16).at[:, :, :lat_w + ROPE_DIM].set(
        w_in[:, :, :lat_w + ROPE_DIM].astype(BF16))
    w_uq = jnp.zeros((n_layers, Q_LORA, N_HEADS, MLA_QK), BF16).at[:, :, :, :HEAD_DIM + ROPE_DIM].set(
        p["w_uq"].reshape(n_layers, Q_LORA, N_HEADS, HEAD_DIM + ROPE_DIM).astype(BF16))
    return dict(p, w_lat=w_lat, w_dil=w_in[:, :, lat_w + ROPE_DIM:].astype(BF16),
                w_uq=w_uq.reshape(n_layers, Q_LORA, N_HEADS * MLA_QK),
                w_ukv=p["w_ukv"].astype(BF16), w_o=p["w_o"].astype(BF16))


def odd_layer(h, p, i, tables, *, batch, seq):
    hn = rmsnorm(h, p["norm_mix"][i], BF16)
    lat = matmul([hn], [(p["w_lat"], 0)], F32, layer=i)
    dqkv = matmul([hn], [(p["w_dil"], 0)], BF16, layer=i)
    cqn = rmsnorm(lat, p["g_cq"][i], BF16, col_block=0, width=Q_LORA)
    ckvn = rmsnorm(lat, p["g_ckv"][i], BF16, col_block=1, width=KV_LORA)
    q_raw = matmul([cqn], [(p["w_uq"], 0)], F32, layer=i)
    kv = matmul([ckvn], [(p["w_ukv"], 0)], BF16, layer=i)
    q_full, k_full = mla_prep(q_raw, lat, kv, tables, seq=seq)
    o_c = flash_attention(q_full, k_full, kv, batch=batch, seq=seq, dqk=MLA_QK,
                          q_col=lambda hd: hd, k_col=lambda hd: hd, v_col=lambda hd: 2 * hd + 1,
                          scale=MLA_SCALE_DIM ** -0.5)
    branches = [dilated_branch(dqkv, batch=batch, seq=seq, rate=r) for r in DIL_RATES]
    o_d = dilated_merge([o for o, _ in branches], [l for _, l in branches])
    h = matmul([o_c, o_d], [(p["w_o"], 0), (p["w_o"], 1)], F32, res=h, layer=i)
    return moe_ffn(h, p["norm_ffn"][i], p["w_router"][i], p["w_exp_gate"], p["w_exp_up"],
                   p["w_exp_down"], layer=i)


def kernel(x, even_norm_mix, even_w_in, even_b_f, even_w_o, even_norm_ffn, even_w_gate, even_w_up,
           even_w_down, odd_norm_mix, odd_w_in, odd_g_cq, odd_g_ckv, odd_w_uq, odd_w_ukv, odd_w_o,
           odd_norm_ffn, odd_w_router, odd_w_exp_gate, odd_w_exp_up, odd_w_exp_down, final_norm):
    batch, seq, d = x.shape
    assert seq % (max(DIL_RATES) * DIL_BLOCK) == 0
    even = prep_even(dict(norm_mix=even_norm_mix, w_in=even_w_in, b_f=even_b_f, w_o=even_w_o,
                          norm_ffn=even_norm_ffn, w_gate=even_w_gate, w_up=even_w_up,
                          w_down=even_w_down))
    odd = prep_odd(dict(norm_mix=odd_norm_mix, w_in=odd_w_in, g_cq=odd_g_cq, g_ckv=odd_g_ckv,
                        w_uq=odd_w_uq, w_ukv=odd_w_ukv, w_o=odd_w_o, norm_ffn=odd_norm_ffn,
                        w_router=odd_w_router, w_exp_gate=odd_w_exp_gate, w_exp_up=odd_w_exp_up,
                        w_exp_down=odd_w_exp_down))
    depth = even_w_in.shape[0] + odd_w_in.shape[0]
    tables = rope_tables(seq)
    h = x.reshape(batch * seq, d)
    for layer in range(depth):
        if layer % 2 == 0:
            h = even_layer(h, even, layer // 2, batch=batch, seq=seq)
        else:
            h = odd_layer(h, odd, layer // 2, tables, batch=batch, seq=seq)
    return rmsnorm(h, final_norm, x.dtype).reshape(batch, seq, d)
```

```python
import functools

import jax
import jax.numpy as jnp
from jax import lax
from jax.experimental import pallas as pl
from jax.experimental.pallas import tpu as pltpu

F32 = jnp.float32
BF16 = jnp.bfloat16

LANE = 128
SUBLANE = 8
HEAD_DIM = 128
N_HEADS = 8
Q_LORA = 512
KV_LORA = 512
ROPE_DIM = 64
MLA_QK = 256
MLA_SCALE_DIM = 192
ROPE_THETA = 10000.0
DIL_RATES = (1, 4, 16)
DIL_BLOCK = 128
N_EXPERTS = 8
NORM_EPS = 1e-6
NEG_INF = float("-inf")
LOG2E = 1.4426950408889634

ROW_TILE = 512
MM_TILE = 1024
ATT_TQ = 512
ATT_TK = 512
SB_TK = 256
FFN_TM = 512
FFN_TF = 512
MOE_TM = 1024
MOE_SUB = 512
MOE_TF = 256
DMA_ROWS = 256
DMA_UNROLL = 4


def _rms(x, g):
    ms = jnp.mean(x * x, axis=-1, keepdims=True)
    return x * lax.rsqrt(ms + NORM_EPS) * g


def _log_sigmoid(x):
    return -(jnp.maximum(-x, 0.0) + jnp.log1p(jnp.exp(-jnp.abs(x))))


def _rmsnorm_body(x_ref, g_ref, o_ref):
    o_ref[...] = _rms(x_ref[...].astype(F32), g_ref[...]).astype(o_ref.dtype)


def rmsnorm(x, g, out_dtype, *, col_block=0, width=None):
    m = x.shape[0]
    width = x.shape[1] if width is None else width
    tm = min(ROW_TILE, m)
    return pl.pallas_call(
        _rmsnorm_body,
        grid=(m // tm,),
        in_specs=[pl.BlockSpec((tm, width), lambda i: (i, col_block)),
                  pl.BlockSpec((1, width), lambda i: (0, 0))],
        out_specs=pl.BlockSpec((tm, width), lambda i: (i, 0)),
        out_shape=jax.ShapeDtypeStruct((m, width), out_dtype),
        name="rmsnorm",
    )(x, g.reshape(1, width).astype(F32))


def _matmul_body(*refs, n_in, has_res):
    o_ref = refs[-1]
    acc = jnp.dot(refs[0][...], refs[n_in][...], preferred_element_type=F32)
    for i in range(1, n_in):
        acc = acc + jnp.dot(refs[i][...], refs[n_in + i][...], preferred_element_type=F32)
    if has_res:
        acc = refs[2 * n_in][...] + acc
    o_ref[...] = acc.astype(o_ref.dtype)


def matmul(a_list, w_list, out_dtype, res=None, *, layer=0, n=None):
    m = a_list[0].shape[0]
    n = w_list[0][0].shape[2] if n is None else n
    tm = min(MM_TILE, m)
    tn = MM_TILE if n % MM_TILE == 0 else n

    def w_spec(a, row_block):
        return pl.BlockSpec((None, a.shape[1], tn), lambda i, j: (layer, row_block, j))

    in_specs = [pl.BlockSpec((tm, a.shape[1]), lambda i, j: (i, 0)) for a in a_list]
    in_specs += [w_spec(a, rb) for a, (_, rb) in zip(a_list, w_list)]
    args = list(a_list) + [w for w, _ in w_list]
    if res is not None:
        in_specs.append(pl.BlockSpec((tm, tn), lambda i, j: (i, j)))
        args.append(res)
    return pl.pallas_call(
        functools.partial(_matmul_body, n_in=len(a_list), has_res=res is not None),
        grid=(m // tm, n // tn),
        in_specs=in_specs,
        out_specs=pl.BlockSpec((tm, tn), lambda i, j: (i, j)),
        out_shape=jax.ShapeDtypeStruct((m, n), out_dtype),
        name="matmul",
    )(*args)


def _out_proj_norm_body(*refs, n_in):
    res_ref, g_ref, h_ref, hn_ref = refs[2 * n_in:]
    acc = jnp.dot(refs[0][...], refs[n_in][...], preferred_element_type=F32)
    for i in range(1, n_in):
        acc = acc + jnp.dot(refs[i][...], refs[n_in + i][...], preferred_element_type=F32)
    h = res_ref[...] + acc
    h_ref[...] = h
    hn_ref[...] = _rms(h, g_ref[...]).astype(hn_ref.dtype)


def out_proj_norm(a_list, w, res, g, *, layer):
    m, d = res.shape
    tm = min(ROW_TILE, m)

    def w_spec(a, row_block):
        return pl.BlockSpec((None, a.shape[1], d), lambda i: (layer, row_block, 0))

    row = pl.BlockSpec((tm, d), lambda i: (i, 0))
    return pl.pallas_call(
        functools.partial(_out_proj_norm_body, n_in=len(a_list)),
        grid=(m // tm,),
        in_specs=[pl.BlockSpec((tm, a.shape[1]), lambda i: (i, 0)) for a in a_list]
        + [w_spec(a, rb) for rb, a in enumerate(a_list)]
        + [row, pl.BlockSpec((1, d), lambda i: (0, 0))],
        out_specs=[row, row],
        out_shape=[jax.ShapeDtypeStruct((m, d), F32), jax.ShapeDtypeStruct((m, d), BF16)],
        name="out_proj_norm",
    )(*a_list, *([w] * len(a_list)), res, g.reshape(1, d).astype(F32))


def _decay_body(fg_ref, b_ref, c_ref, *, seq):
    bias = b_ref[...]
    row = lax.broadcasted_iota(jnp.int32, (SUBLANE, LANE), 0)

    def body(i, carry):
        st = pl.multiple_of(i * SUBLANE, SUBLANE)
        x = _log_sigmoid(fg_ref[pl.ds(st, SUBLANE), :] + bias)
        for s in (1, 2, 4):
            x = x + jnp.where(row >= s, pltpu.roll(x, s, 0), 0.0)
        x = x + carry
        c_ref[pl.ds(st, SUBLANE), :] = x
        return jnp.broadcast_to(x[SUBLANE - 1:SUBLANE, :], (SUBLANE, LANE))

    lax.fori_loop(0, seq // SUBLANE, body, jnp.zeros((SUBLANE, LANE), F32), unroll=8)


def decay_cumsum(fg, b_f, batch, seq):
    bias = jnp.zeros((1, LANE), F32).at[0, :b_f.shape[0]].set(b_f.astype(F32))
    return pl.pallas_call(
        functools.partial(_decay_body, seq=seq),
        grid=(batch,),
        in_specs=[pl.BlockSpec((seq, LANE), lambda b: (b, 0)),
                  pl.BlockSpec((1, LANE), lambda b: (0, 0))],
        out_specs=pl.BlockSpec((seq, LANE), lambda b: (b, 0)),
        out_shape=jax.ShapeDtypeStruct((batch * seq, LANE), F32),
        name="decay_cumsum",
    )(fg, bias)


def _flash_body(*refs, tq, tk, scale, has_decay):
    if has_decay:
        q_ref, k_ref, v_ref, ccol_ref, crow_ref, o_ref = refs
    else:
        q_ref, k_ref, v_ref, o_ref = refs
    h = pl.program_id(1)
    qi = pl.program_id(2)
    q = q_ref[...]
    to_log2 = scale * LOG2E
    if has_decay:
        cblk = ccol_ref[...]
        lane = lax.broadcasted_iota(jnp.int32, cblk.shape, 1)
        cq = jnp.sum(jnp.where(lane == h, cblk, 0.0), axis=-1, keepdims=True) * LOG2E

    def span(kb, n, carry, masked):
        m, l, acc = carry
        start = pl.multiple_of(kb * tk, tk)
        k = k_ref[pl.ds(start, n * tk), :]
        t_all = lax.dot_general(q, k, (((1,), (1,)), ((), ())), preferred_element_type=F32) * to_log2
        if has_decay:
            t_all = t_all - crow_ref[:, pl.ds(start, n * tk)] * LOG2E
        for j in range(n):
            t = t_all[:, j * tk:(j + 1) * tk]
            if masked:
                row = lax.broadcasted_iota(jnp.int32, (tq, tk), 0)
                col = lax.broadcasted_iota(jnp.int32, (tq, tk), 1)
                t = jnp.where(col <= row, t, NEG_INF)
            row_max = jnp.max(t, axis=-1, keepdims=True)
            if has_decay:
                row_max = row_max + cq
            m_new = jnp.maximum(m, row_max)
            shift = m_new - cq if has_decay else m_new
            alpha = jnp.exp2(m - m_new)
            p = jnp.exp2(t - shift)
            l = alpha * l + jnp.sum(p, axis=-1, keepdims=True)
            v = v_ref[pl.ds(pl.multiple_of(start + j * tk, tk), tk), :]
            acc = alpha * acc + jnp.dot(p.astype(BF16), v, preferred_element_type=F32)
            m = m_new
        return m, l, acc

    init = (jnp.full((tq, 1), NEG_INF, F32), jnp.zeros((tq, 1), F32),
            jnp.zeros((tq, v_ref.shape[1]), F32))
    carry = lax.fori_loop(0, qi // 2, lambda i, c: span(2 * i, 2, c, False), init)
    carry = lax.cond(qi % 2 == 1, lambda c: span(qi - 1, 1, c, False), lambda c: c, carry)
    m, l, acc = span(qi, 1, carry, True)
    o_ref[...] = (acc / l).astype(o_ref.dtype)


def flash_attention(q_arr, k_arr, v_arr, *, batch, seq, dqk, q_col, k_col, v_col, scale,
                    decay=None):
    tq = tk = min(ATT_TQ, seq)
    nq = seq // tq
    in_specs = [pl.BlockSpec((tq, dqk), lambda b, h, i: (b * nq + i, q_col(h))),
                pl.BlockSpec((seq, dqk), lambda b, h, i: (b, k_col(h))),
                pl.BlockSpec((seq, HEAD_DIM), lambda b, h, i: (b, v_col(h)))]
    args = [q_arr, k_arr, v_arr]
    if decay is not None:
        in_specs += [pl.BlockSpec((tq, LANE), lambda b, h, i: (b * nq + i, 0)),
                     pl.BlockSpec((None, None, 1, seq), lambda b, h, i: (b, h, 0, 0))]
        args += list(decay)
    return pl.pallas_call(
        functools.partial(_flash_body, tq=tq, tk=tk, scale=scale, has_decay=decay is not None),
        grid=(batch, N_HEADS, nq),
        in_specs=in_specs,
        out_specs=pl.BlockSpec((tq, HEAD_DIM), lambda b, h, i: (b * nq + i, h)),
        out_shape=jax.ShapeDtypeStruct((batch * seq, N_HEADS * HEAD_DIM), BF16),
        name="flash_attention",
    )(*args)


def _sb_body(q_ref, k_ref, v_ref, o_ref, *, tq, tk, scale):
    qi = pl.program_id(2)
    q = q_ref[...]
    jrow = lax.broadcasted_iota(jnp.int32, (tk, tk), 0)
    scol = lax.broadcasted_iota(jnp.int32, (tk, tk), 1)
    later = jnp.where(jrow > scol, 1.0, 0.0).astype(BF16)
    later2 = jnp.concatenate([later, later], axis=0)

    def span(q_block, n_q, carry, masked):
        run, acc = carry
        width = n_q * tq
        base = pl.multiple_of(q_block * tq, tq)
        k = k_ref[pl.ds(base, width), :]
        v = v_ref[pl.ds(base, width), :]
        z = lax.dot_general(q, k, (((1,), (1,)), ((), ())), preferred_element_type=F32) * (scale * LOG2E)
        sp = jnp.maximum(z, 0.0) + jnp.log2(1.0 + jnp.exp2(-jnp.abs(z)))
        l1m = -sp
        if masked:
            strict = (lax.broadcasted_iota(jnp.int32, (tq, width), 1)
                      < lax.broadcasted_iota(jnp.int32, (tq, width), 0))
            l1m = jnp.where(strict, l1m, 0.0)
        hi = l1m.astype(BF16)
        lo = (l1m - hi.astype(F32)).astype(BF16)
        blocks = [slice(j * tk, (j + 1) * tk) for j in range(width // tk)]
        after = [jnp.dot(jnp.concatenate([hi[:, b], lo[:, b]], axis=1), later2,
                         preferred_element_type=F32) for b in blocks]
        sums = [jnp.sum(l1m[:, b], axis=-1, keepdims=True) for b in blocks]
        logw = []
        for j in reversed(range(len(blocks))):
            logw.append((z[:, blocks[j]] - sp[:, blocks[j]]) + after[j] + run)
            run = run + sums[j]
        a = jnp.exp2(jnp.concatenate(logw[::-1], axis=1))
        if masked:
            a = jnp.where(strict, a, 0.0)
        acc = acc + jnp.dot(a.astype(BF16), v, preferred_element_type=F32)
        return run, acc

    carry = (jnp.zeros((tq, 1), F32), jnp.zeros((tq, HEAD_DIM), F32))
    carry = span(qi, 1, carry, True)
    carry = lax.cond(qi % 2 == 1, lambda c: span(qi - 1, 1, c, False), lambda c: c, carry)
    n_pairs = qi // 2
    _, acc = lax.fori_loop(0, n_pairs, lambda i, c: span(2 * (n_pairs - 1 - i), 2, c, False), carry)
    o_ref[...] = acc.astype(o_ref.dtype)


def stick_breaking_attention(qkv, *, batch, seq, q_col, k_col, v_col):
    tq = min(ATT_TQ, seq)
    tk = min(SB_TK, tq)
    nq = seq // tq
    return pl.pallas_call(
        functools.partial(_sb_body, tq=tq, tk=tk, scale=HEAD_DIM ** -0.5),
        grid=(batch, N_HEADS, nq),
        in_specs=[pl.BlockSpec((tq, HEAD_DIM), lambda b, h, i: (b * nq + i, q_col(h))),
                  pl.BlockSpec((seq, HEAD_DIM), lambda b, h, i: (b, k_col(h))),
                  pl.BlockSpec((seq, HEAD_DIM), lambda b, h, i: (b, v_col(h)))],
        out_specs=pl.BlockSpec((tq, HEAD_DIM), lambda b, h, i: (b * nq + i, h)),
        out_shape=jax.ShapeDtypeStruct((batch * seq, N_HEADS * HEAD_DIM), BF16),
        name="stick_breaking",
    )(qkv, qkv, qkv)


def _mla_prep_body(q_ref, kr_ref, kv_ref, cos_ref, sin_lo_ref, sin_hi_ref, qf_ref, kf_ref):
    cos, sin_lo, sin_hi = cos_ref[...], sin_lo_ref[...], sin_hi_ref[...]

    def rope(x):
        return x * cos + pltpu.roll(x, LANE - ROPE_DIM // 2, 1) * sin_lo \
            + pltpu.roll(x, ROPE_DIM // 2, 1) * sin_hi

    k_pe = rope(kr_ref[...]).astype(BF16)
    for h in range(N_HEADS):
        lo = h * MLA_QK
        qf_ref[:, lo:lo + HEAD_DIM] = q_ref[:, lo:lo + HEAD_DIM].astype(BF16)
        qf_ref[:, lo + HEAD_DIM:lo + MLA_QK] = rope(q_ref[:, lo + HEAD_DIM:lo + MLA_QK]).astype(BF16)
        kf_ref[:, lo:lo + HEAD_DIM] = kv_ref[:, lo:lo + HEAD_DIM]
        kf_ref[:, lo + HEAD_DIM:lo + MLA_QK] = k_pe


def mla_prep(q_raw, lat, kv, tables, *, seq):
    m = q_raw.shape[0]
    tm = min(ROW_TILE, seq)
    npos = seq // tm
    wide = N_HEADS * MLA_QK
    kr_block = (Q_LORA + KV_LORA) // LANE
    tab_spec = pl.BlockSpec((tm, LANE), lambda i: (i % npos, 0))
    return pl.pallas_call(
        _mla_prep_body,
        grid=(m // tm,),
        in_specs=[pl.BlockSpec((tm, wide), lambda i: (i, 0)),
                  pl.BlockSpec((tm, LANE), lambda i: (i, kr_block)),
                  pl.BlockSpec((tm, wide), lambda i: (i, 0)),
                  tab_spec, tab_spec, tab_spec],
        out_specs=[pl.BlockSpec((tm, wide), lambda i: (i, 0)),
                   pl.BlockSpec((tm, wide), lambda i: (i, 0))],
        out_shape=[jax.ShapeDtypeStruct((m, wide), BF16),
                   jax.ShapeDtypeStruct((m, wide), BF16)],
        name="mla_prep",
    )(q_raw, lat, kv, *tables)


def rope_tables(seq):
    half = ROPE_DIM // 2
    inv_freq = ROPE_THETA ** (-jnp.arange(half, dtype=F32) / half)
    ang = jnp.arange(seq, dtype=F32)[:, None] * inv_freq[None, :]
    cos, sin = jnp.cos(ang), jnp.sin(ang)
    zeros = jnp.zeros((seq, half), F32)
    pad = jnp.zeros((seq, LANE - ROPE_DIM), F32)
    cos_t = jnp.concatenate([cos, cos, pad], axis=1)
    sin_lo = jnp.concatenate([-sin, zeros, pad], axis=1)
    sin_hi = jnp.concatenate([zeros, sin, pad], axis=1)
    return cos_t, sin_lo, sin_hi


def _dilated_body(q_ref, kp_ref, kc_ref, vp_ref, vc_ref, o_ref, lse_ref, *, rate, scale):
    ib = pl.program_id(2)
    n = DIL_BLOCK
    qi = lax.broadcasted_iota(jnp.int32, (n, 2 * n), 0)
    ki = lax.broadcasted_iota(jnp.int32, (n, 2 * n), 1)
    steps = n + qi - ki
    valid = (steps >= 0) & (steps <= n) & ((ki >= n) | (ib > 0))
    dist = (steps * rate).astype(F32)
    lane = lax.broadcasted_iota(jnp.int32, (n, LANE), 1)
    lse_all = jnp.zeros((n, LANE), F32)
    for h in range(N_HEADS):
        cols = slice(h * HEAD_DIM, (h + 1) * HEAD_DIM)
        q = q_ref[:, cols]
        k = jnp.concatenate([kp_ref[:, cols], kc_ref[:, cols]], axis=0)
        v = jnp.concatenate([vp_ref[:, cols], vc_ref[:, cols]], axis=0)
        s = lax.dot_general(q, k, (((1,), (1,)), ((), ())), preferred_element_type=F32) * scale
        s = jnp.where(valid, s - (2.0 ** -(h + 1)) * dist, NEG_INF)
        m = jnp.max(s, axis=-1, keepdims=True)
        p = jnp.exp(s - m)
        l = jnp.sum(p, axis=-1, keepdims=True)
        o = jnp.dot(p.astype(BF16), v, preferred_element_type=F32) / l
        o_ref[:, cols] = o.astype(o_ref.dtype)
        lse_all = jnp.where(lane == h, m + jnp.log(l), lse_all)
    lse_ref[...] = lse_all


def _dilated_proj_body(a_ref, w_ref, *refs):
    out_refs, acc_ref = refs[:-1], refs[-1]
    acc = jnp.dot(a_ref[...], w_ref[...], preferred_element_type=F32)
    n_slabs, tm, _ = acc_ref.shape
    for j in range(n_slabs):
        acc_ref[j] = acc[:, j * LANE:(j + 1) * LANE]
    for rate, o_ref in zip(DIL_RATES, out_refs):
        if rate == 1:
            o_ref[...] = acc.astype(o_ref.dtype)
            continue
        for c in range(rate):
            for j in range(n_slabs):
                o_ref[c, :, j * LANE:(j + 1) * LANE] = (
                    acc_ref[j, pl.ds(c, tm // rate, stride=rate), :].astype(o_ref.dtype))


def dilated_proj(a, w, *, layer):
    m, k = a.shape
    n = w.shape[2]
    tm, tn = min(MM_TILE, m), MM_TILE

    def out_spec(rate):
        if rate == 1:
            return pl.BlockSpec((tm, tn), lambda i, j: (i, j))
        return pl.BlockSpec((rate, tm // rate, tn), lambda i, j: (0, i, j))

    def out_shape(rate):
        return jax.ShapeDtypeStruct((m, n) if rate == 1 else (rate, m // rate, n), BF16)

    return pl.pallas_call(
        _dilated_proj_body,
        grid=(m // tm, n // tn),
        in_specs=[pl.BlockSpec((tm, k), lambda i, j: (i, 0)),
                  pl.BlockSpec((None, k, tn), lambda i, j: (layer, 0, j))],
        out_specs=[out_spec(r) for r in DIL_RATES],
        out_shape=[out_shape(r) for r in DIL_RATES],
        scratch_shapes=[pltpu.VMEM((tn // LANE, tm, LANE), F32)],
        name="dilated_proj",
    )(a, w)


def dilated_branch(dqkv, *, batch, seq, rate):
    wide = N_HEADS * HEAD_DIM
    n_rows = batch * seq
    nb = seq // rate // DIL_BLOCK
    rows = dqkv.reshape(n_rows, 3 * wide)

    def row_block(b, c, i):
        return (c * batch + b) * nb + i

    def spec(which, prev):
        def index(b, c, i):
            return (row_block(b, c, jnp.maximum(i - 1, 0) if prev else i), which)
        return pl.BlockSpec((DIL_BLOCK, wide), index)

    o, lse = pl.pallas_call(
        functools.partial(_dilated_body, rate=rate, scale=HEAD_DIM ** -0.5),
        grid=(batch, rate, nb),
        in_specs=[spec(0, False), spec(1, True), spec(1, False), spec(2, True), spec(2, False)],
        out_specs=[pl.BlockSpec((DIL_BLOCK, wide), lambda b, c, i: (row_block(b, c, i), 0)),
                   pl.BlockSpec((DIL_BLOCK, LANE), lambda b, c, i: (row_block(b, c, i), 0))],
        out_shape=[jax.ShapeDtypeStruct((n_rows, wide), BF16),
                   jax.ShapeDtypeStruct((n_rows, LANE), F32)],
        name="dilated_branch",
    )(rows, rows, rows, rows, rows)
    if rate == 1:
        return o, lse
    return o.reshape(rate, n_rows // rate, wide), lse.reshape(rate, n_rows // rate, LANE)


def _dilated_merge_body(*refs):
    n = len(DIL_RATES)
    o_refs, l_refs, out_ref = refs[:n], refs[n:2 * n], refs[2 * n]
    scratch = refs[2 * n + 1:]
    outs, lses = [], []
    for g, rate in enumerate(DIL_RATES):
        if rate == 1:
            outs.append(o_refs[g][...].astype(F32))
            lses.append(l_refs[g][...])
            continue
        o_nat, l_nat = scratch[2 * (g - 1)], scratch[2 * (g - 1) + 1]
        rows = l_nat.shape[0] // rate
        for c in range(rate):
            for h in range(N_HEADS):
                o_nat[h, pl.ds(c, rows, stride=rate), :] = (
                    o_refs[g][c, :, h * HEAD_DIM:(h + 1) * HEAD_DIM].astype(F32))
            l_nat[pl.ds(c, rows, stride=rate), :] = l_refs[g][c]
        outs.append(o_nat)
        lses.append(l_nat[...])
    top = functools.reduce(jnp.maximum, lses)
    es = [jnp.exp(x - top) for x in lses]
    den = functools.reduce(lambda a, b: a + b, es)
    ws = [e / den for e in es]
    for h in range(N_HEADS):
        cols = slice(h * HEAD_DIM, (h + 1) * HEAD_DIM)
        acc = ws[0][:, h:h + 1] * outs[0][:, cols]
        for g in range(1, n):
            acc = acc + ws[g][:, h:h + 1] * outs[g][h]
        out_ref[:, cols] = acc.astype(out_ref.dtype)


def dilated_merge(outs, lses):
    m = outs[0].shape[0]
    wide = N_HEADS * HEAD_DIM
    tm = min(ROW_TILE, m)

    def spec(rate, width):
        if rate == 1:
            return pl.BlockSpec((tm, width), lambda i: (i, 0))
        return pl.BlockSpec((rate, tm // rate, width), lambda i: (0, i, 0))

    scratch = []
    for rate in DIL_RATES[1:]:
        scratch += [pltpu.VMEM((N_HEADS, tm, HEAD_DIM), F32), pltpu.VMEM((tm, LANE), F32)]
    return pl.pallas_call(
        _dilated_merge_body,
        grid=(m // tm,),
        in_specs=[spec(r, wide) for r in DIL_RATES] + [spec(r, LANE) for r in DIL_RATES],
        out_specs=pl.BlockSpec((tm, wide), lambda i: (i, 0)),
        out_shape=jax.ShapeDtypeStruct((m, wide), BF16),
        scratch_shapes=scratch,
        name="dilated_merge",
    )(*outs, *lses)


def _swiglu_partial(x, wg, wu, wd):
    g = jnp.dot(x, wg, preferred_element_type=F32)
    u = jnp.dot(x, wu, preferred_element_type=F32)
    mid = (g * (1.0 / (1.0 + jnp.exp(-g))) * u).astype(BF16)
    return jnp.dot(mid, wd, preferred_element_type=F32)


def _dense_ffn_body(x_ref, wg_ref, wu_ref, wd_ref, res_ref, g_ref, o_ref, on_ref):
    f = pl.program_id(1)

    @pl.when(f == 0)
    def _():
        o_ref[...] = res_ref[...]

    o_ref[...] += _swiglu_partial(x_ref[...], wg_ref[...], wu_ref[...], wd_ref[...])

    @pl.when(f == pl.num_programs(1) - 1)
    def _():
        on_ref[...] = _rms(o_ref[...], g_ref[...]).astype(on_ref.dtype)


def dense_ffn(x, w_gate, w_up, w_down, res, g_next, *, layer):
    m, d = x.shape
    ff = w_gate.shape[2]
    tm, tf = min(FFN_TM, m), min(FFN_TF, ff)
    row = pl.BlockSpec((tm, d), lambda i, f: (i, 0))
    return pl.pallas_call(
        _dense_ffn_body,
        grid=(m // tm, ff // tf),
        in_specs=[row,
                  pl.BlockSpec((None, d, tf), lambda i, f: (layer, 0, f)),
                  pl.BlockSpec((None, d, tf), lambda i, f: (layer, 0, f)),
                  pl.BlockSpec((None, tf, d), lambda i, f: (layer, f, 0)),
                  row,
                  pl.BlockSpec((1, d), lambda i, f: (0, 0))],
        out_specs=[row, row],
        out_shape=[jax.ShapeDtypeStruct((m, d), F32), jax.ShapeDtypeStruct((m, d), BF16)],
        name="dense_ffn",
    )(x, w_gate, w_up, w_down, res, g_next.reshape(1, d).astype(F32))


def _pack_bf16_pairs(lo, hi):
    lo_bits = lax.bitcast_convert_type(lo.astype(BF16).astype(F32), jnp.uint32)
    hi_bits = lax.bitcast_convert_type(hi.astype(BF16).astype(F32), jnp.uint32)
    return (lo_bits >> 16) | (hi_bits & jnp.uint32(0xFFFF0000))


def _unpack_bf16_pairs(packed):
    lo = lax.bitcast_convert_type(packed << 16, F32).astype(BF16)
    hi = lax.bitcast_convert_type(packed & jnp.uint32(0xFFFF0000), F32).astype(BF16)
    return lo, hi


def _grouped_ffn_body(meta_ref, x_ref, wg_ref, wu_ref, wd_ref, o_ref, xb_ref, *, n_tiles, sub):
    i = pl.program_id(0)
    f = pl.program_id(1)
    in_use = i < meta_ref[n_tiles]
    rows = meta_ref[n_tiles + 1 + i]
    half = x_ref.shape[1]

    @pl.when(f == 0)
    def _():
        o_ref[...] = jnp.zeros_like(o_ref)

    @pl.when(in_use & (f == 0))
    def _():
        lo, hi = _unpack_bf16_pairs(x_ref[...])
        xb_ref[:, :half] = lo
        xb_ref[:, half:] = hi

    for s in range(x_ref.shape[0] // sub):
        @pl.when(in_use & (rows > s * sub))
        def _():
            sl = pl.ds(s * sub, sub)
            o_ref[sl, :] += _swiglu_partial(xb_ref[sl, :], wg_ref[...].astype(BF16),
                                            wu_ref[...].astype(BF16), wd_ref[...].astype(BF16))


def grouped_ffn(xs, meta, w_gate, w_up, w_down, *, layer, tm):
    ns, half = xs.shape
    d = 2 * half
    n_tiles = ns // tm
    ff = w_gate.shape[3]
    tf = min(MOE_TF, ff)
    nf = ff // tf

    def tile(i, meta):
        return jnp.maximum(jnp.minimum(i, meta[n_tiles] - 1), 0)

    def expert(i, meta):
        return meta[tile(i, meta)]

    def chunk(i, f, meta):
        return jnp.where(i < meta[n_tiles], f, nf - 1)

    grid_spec = pltpu.PrefetchScalarGridSpec(
        num_scalar_prefetch=1,
        grid=(n_tiles, nf),
        in_specs=[pl.BlockSpec((tm, half), lambda i, f, meta: (tile(i, meta), 0)),
                  pl.BlockSpec((None, None, d, tf),
                               lambda i, f, meta: (layer, expert(i, meta), 0, chunk(i, f, meta))),
                  pl.BlockSpec((None, None, d, tf),
                               lambda i, f, meta: (layer, expert(i, meta), 0, chunk(i, f, meta))),
                  pl.BlockSpec((None, None, tf, d),
                               lambda i, f, meta: (layer, expert(i, meta), chunk(i, f, meta), 0))],
        out_specs=pl.BlockSpec((tm, d), lambda i, f, meta: (i, 0)),
        scratch_shapes=[pltpu.VMEM((tm, d), BF16)],
    )
    return pl.pallas_call(
        functools.partial(_grouped_ffn_body, n_tiles=n_tiles, sub=min(MOE_SUB, tm)),
        grid_spec=grid_spec,
        out_shape=jax.ShapeDtypeStruct((ns, d), F32),
        name="grouped_ffn",
    )(meta, xs, w_gate, w_up, w_down)


ROUTE_I1, ROUTE_I2, ROUTE_G1, ROUTE_G2, ROUTE_P1, ROUTE_P2 = range(6)


def _router_body(h_ref, g_ref, wr_ref, route_ref, cnt_ref, seen_ref, *, tr):
    @pl.when(pl.program_id(0) == 0)
    def _():
        seen_ref[...] = jnp.zeros_like(seen_ref)

    y = _rms(h_ref[...], g_ref[...])
    logits = jnp.dot(y, wr_ref[...], preferred_element_type=F32, precision=lax.Precision.HIGHEST)
    lane = lax.broadcasted_iota(jnp.int32, (tr, LANE), 1)
    l_a = jnp.where(lane < N_EXPERTS, logits, NEG_INF)
    m1 = jnp.max(l_a, axis=-1, keepdims=True)
    i1 = jnp.min(jnp.where(l_a == m1, lane, LANE), axis=-1, keepdims=True)
    l_b = jnp.where(lane == i1, NEG_INF, l_a)
    m2 = jnp.max(l_b, axis=-1, keepdims=True)
    i2 = jnp.min(jnp.where(l_b == m2, lane, LANE), axis=-1, keepdims=True)
    e = jnp.exp(m2 - m1)
    g1 = 1.0 / (1.0 + e)
    g2 = e / (1.0 + e)
    sel = jnp.where((lane == i1) | (lane == i2), 1.0, 0.0)
    r = lax.broadcasted_iota(jnp.int32, (tr, tr), 0)
    c = lax.broadcasted_iota(jnp.int32, (tr, tr), 1)
    earlier = jnp.where(r > c, 1.0, 0.0).astype(BF16)
    rank = jnp.dot(earlier, sel.astype(BF16), preferred_element_type=F32) + seen_ref[0:1, :]
    p1 = jnp.sum(jnp.where(lane == i1, rank, 0.0), axis=-1, keepdims=True)
    p2 = jnp.sum(jnp.where(lane == i2, rank, 0.0), axis=-1, keepdims=True)
    seen = seen_ref[...] + jnp.sum(sel, axis=0, keepdims=True)
    seen_ref[...] = seen
    cnt_ref[...] = seen
    route = jnp.zeros((tr, LANE), F32)
    for idx, val in ((ROUTE_I1, i1.astype(F32)), (ROUTE_I2, i2.astype(F32)), (ROUTE_G1, g1),
                     (ROUTE_G2, g2), (ROUTE_P1, p1), (ROUTE_P2, p2)):
        route = jnp.where(lane == idx, val, route)
    route_ref[...] = route


def router(h, g, w_router):
    m, d = h.shape
    tr = min(ROW_TILE, m)
    wr = jnp.zeros((d, LANE), F32).at[:, :N_EXPERTS].set(w_router.astype(F32))
    return pl.pallas_call(
        functools.partial(_router_body, tr=tr),
        grid=(m // tr,),
        in_specs=[pl.BlockSpec((tr, d), lambda i: (i, 0)),
                  pl.BlockSpec((1, d), lambda i: (0, 0)),
                  pl.BlockSpec((d, LANE), lambda i: (0, 0))],
        out_specs=[pl.BlockSpec((tr, LANE), lambda i: (i, 0)),
                   pl.BlockSpec((SUBLANE, LANE), lambda i: (0, 0))],
        out_shape=[jax.ShapeDtypeStruct((m, LANE), F32),
                   jax.ShapeDtypeStruct((SUBLANE, LANE), F32)],
        scratch_shapes=[pltpu.VMEM((SUBLANE, LANE), F32)],
        name="router",
    )(h, g.reshape(1, d).astype(F32), wr)


def _row_copies(copies, step, slot, tr, wait):
    def body(r, _):
        for queue, cp in enumerate(copies(step, slot, r)):
            if wait:
                cp.wait()
            else:
                cp.start(priority=queue)
        return 0

    lax.fori_loop(0, tr, body, 0, unroll=DMA_UNROLL)


def _scatter_body(s1_ref, s2_ref, h_ref, g_ref, xs_in_ref, xs_ref, buf_ref, sems, *, tr):
    del xs_in_ref
    i = pl.program_id(0)
    slot = i % 2
    y = _rms(h_ref[...], g_ref[...])
    half = y.shape[1] // 2
    buf_ref[slot] = _pack_bf16_pairs(y[:, :half], y[:, half:])

    def copies(step, slot, r):
        src = buf_ref.at[slot, pl.ds(r, 1)]
        t = step * tr + r
        return (pltpu.make_async_copy(src, xs_ref.at[pl.ds(s1_ref[t], 1)], sems.at[slot]),
                pltpu.make_async_copy(src, xs_ref.at[pl.ds(s2_ref[t], 1)], sems.at[slot]))

    _row_copies(copies, i, slot, tr, wait=False)

    @pl.when(i > 0)
    def _():
        _row_copies(copies, i - 1, 1 - slot, tr, wait=True)

    @pl.when(i == pl.num_programs(0) - 1)
    def _():
        _row_copies(copies, i, slot, tr, wait=True)


def scatter_rows(h, g, slot1, slot2, n_slots):
    m, d = h.shape
    tr = min(DMA_ROWS, m)
    grid_spec = pltpu.PrefetchScalarGridSpec(
        num_scalar_prefetch=2,
        grid=(m // tr,),
        in_specs=[pl.BlockSpec((tr, d), lambda i, s1, s2: (i, 0)),
                  pl.BlockSpec((1, d), lambda i, s1, s2: (0, 0)),
                  pl.BlockSpec(memory_space=pl.ANY)],
        out_specs=pl.BlockSpec(memory_space=pl.ANY),
        scratch_shapes=[pltpu.VMEM((2, tr, d // 2), jnp.uint32), pltpu.SemaphoreType.DMA((2,))],
    )
    return pl.pallas_call(
        functools.partial(_scatter_body, tr=tr),
        grid_spec=grid_spec,
        out_shape=jax.ShapeDtypeStruct((n_slots, d // 2), jnp.uint32),
        input_output_aliases={4: 0},
        compiler_params=pltpu.CompilerParams(dimension_semantics=("arbitrary",)),
        name="scatter_rows",
    )(slot1, slot2, h, g.reshape(1, d).astype(F32), jnp.zeros((n_slots, d // 2), jnp.uint32))


def _combine_body(s1_ref, s2_ref, h_ref, route_ref, g_ref, y_ref, o_ref, on_ref, buf_ref, sems, *, tr):
    i = pl.program_id(0)
    slot = i % 2

    def copies(step, slot, r):
        t = step * tr + r
        return (pltpu.make_async_copy(y_ref.at[pl.ds(s1_ref[t], 1)], buf_ref.at[slot, 0, pl.ds(r, 1)],
                                      sems.at[slot]),
                pltpu.make_async_copy(y_ref.at[pl.ds(s2_ref[t], 1)], buf_ref.at[slot, 1, pl.ds(r, 1)],
                                      sems.at[slot]))

    @pl.when(i == 0)
    def _():
        _row_copies(copies, 0, 0, tr, wait=False)

    @pl.when(i + 1 < pl.num_programs(0))
    def _():
        _row_copies(copies, i + 1, 1 - slot, tr, wait=False)

    _row_copies(copies, i, slot, tr, wait=True)
    route = route_ref[...]
    g1 = route[:, ROUTE_G1:ROUTE_G1 + 1]
    g2 = route[:, ROUTE_G2:ROUTE_G2 + 1]
    h = h_ref[...] + (g1 * buf_ref[slot, 0] + g2 * buf_ref[slot, 1])
    o_ref[...] = h
    on_ref[...] = _rms(h, g_ref[...]).astype(on_ref.dtype)


def combine_rows(h, route, y, slot1, slot2, g_next, norm_dtype):
    m, d = h.shape
    tr = min(DMA_ROWS, m)
    row = pl.BlockSpec((tr, d), lambda i, s1, s2: (i, 0))
    grid_spec = pltpu.PrefetchScalarGridSpec(
        num_scalar_prefetch=2,
        grid=(m // tr,),
        in_specs=[row,
                  pl.BlockSpec((tr, LANE), lambda i, s1, s2: (i, 0)),
                  pl.BlockSpec((1, d), lambda i, s1, s2: (0, 0)),
                  pl.BlockSpec(memory_space=pl.ANY)],
        out_specs=[row, row],
        scratch_shapes=[pltpu.VMEM((2, 2, tr, d), F32), pltpu.SemaphoreType.DMA((2,))],
    )
    return pl.pallas_call(
        functools.partial(_combine_body, tr=tr),
        grid_spec=grid_spec,
        out_shape=[jax.ShapeDtypeStruct((m, d), F32), jax.ShapeDtypeStruct((m, d), norm_dtype)],
        compiler_params=pltpu.CompilerParams(dimension_semantics=("arbitrary",)),
        name="combine_rows",
    )(slot1, slot2, h, route, g_next.reshape(1, d).astype(F32), y)


def moe_ffn(h, g, w_router, w_gate, w_up, w_down, g_next, norm_dtype, *, layer):
    m, _ = h.shape
    tm = min(MOE_TM, m)
    n_tiles = -(-(2 * m) // tm) + N_EXPERTS
    route, counts = router(h, g, w_router)
    col = lambda j: route[:, j].astype(jnp.int32)
    cnt = counts[0, :N_EXPERTS].astype(jnp.int32)
    tiles_per = (cnt + tm - 1) // tm
    tile_end = jnp.cumsum(tiles_per)
    tile_start = tile_end - tiles_per
    group_start = tile_start * tm
    slot1 = group_start[col(ROUTE_I1)] + col(ROUTE_P1)
    slot2 = group_start[col(ROUTE_I2)] + col(ROUTE_P2)
    tile_ids = jnp.arange(n_tiles)
    tile_expert = jnp.minimum(jnp.sum(tile_ids[:, None] >= tile_end[None, :], axis=1), N_EXPERTS - 1)
    tile_rows = jnp.clip(cnt[tile_expert] - (tile_ids - tile_start[tile_expert]) * tm, 0, tm)
    meta = jnp.concatenate([tile_expert, tile_end[-1:], tile_rows]).astype(jnp.int32)
    xs = scatter_rows(h, g, slot1, slot2, n_tiles * tm)
    ys = grouped_ffn(xs, meta, w_gate, w_up, w_down, layer=layer, tm=tm)
    return combine_rows(h, route, ys, slot1, slot2, g_next, norm_dtype)


def prep_even(p):
    wide = N_HEADS * HEAD_DIM
    w_in = p["w_in"]
    n_layers, d, _ = w_in.shape
    w_fg = jnp.zeros((n_layers, d, LANE), BF16).at[:, :, :N_HEADS].set(w_in[:, :, 6 * wide:].astype(BF16))
    return dict(p, w_qkv=w_in[:, :, :6 * wide].astype(BF16), w_fg=w_fg, w_o=p["w_o"].astype(BF16),
                w_gate=p["w_gate"].astype(BF16), w_up=p["w_up"].astype(BF16),
                w_down=p["w_down"].astype(BF16))


def even_layer(h, hn, p, i, g_next, *, batch, seq):
    qkv = matmul([hn], [(p["w_qkv"], 0)], BF16, layer=i)
    fg = matmul([hn], [(p["w_fg"], 0)], F32, layer=i)
    c = decay_cumsum(fg, p["b_f"][i], batch, seq)
    c_row = c[:, :N_HEADS].reshape(batch, seq, N_HEADS).transpose(0, 2, 1).reshape(batch, N_HEADS, 1, seq)
    o_a = flash_attention(qkv, qkv, qkv, batch=batch, seq=seq, dqk=HEAD_DIM,
                          q_col=lambda hd: hd, k_col=lambda hd: N_HEADS + hd,
                          v_col=lambda hd: 2 * N_HEADS + hd, scale=HEAD_DIM ** -0.5,
                          decay=(c, c_row))
    o_b = stick_breaking_attention(qkv, batch=batch, seq=seq, q_col=lambda hd: 3 * N_HEADS + hd,
                                   k_col=lambda hd: 4 * N_HEADS + hd,
                                   v_col=lambda hd: 5 * N_HEADS + hd)
    h, hn = out_proj_norm([o_a, o_b], p["w_o"], h, p["norm_ffn"][i], layer=i)
    return dense_ffn(hn, p["w_gate"], p["w_up"], p["w_down"], h, g_next, layer=i)


def prep_odd(p):
    w_in = p["w_in"]
    n_layers, d, _ = w_in.shape
    lat_w = Q_LORA + KV_LORA
    w_lat = jnp.zeros((n_layers, d, lat_w + LANE), BF16).at[:, :, :lat_w + ROPE_DIM].set(
        w_in[:, :, :lat_w + ROPE_DIM].astype(BF16))
    w_uq = jnp.zeros((n_layers, Q_LORA, N_HEADS, MLA_QK), BF16).at[:, :, :, :HEAD_DIM + ROPE_DIM].set(
        p["w_uq"].reshape(n_layers, Q_LORA, N_HEADS, HEAD_DIM + ROPE_DIM).astype(BF16))
    return dict(p, w_lat=w_lat, w_dil=w_in[:, :, lat_w + ROPE_DIM:].astype(BF16),
                w_uq=w_uq.reshape(n_layers, Q_LORA, N_HEADS * MLA_QK),
                w_ukv=p["w_ukv"].astype(BF16), w_o=p["w_o"].astype(BF16))


def odd_layer(h, hn, p, i, tables, g_next, norm_dtype, *, batch, seq):
    lat = matmul([hn], [(p["w_lat"], 0)], F32, layer=i)
    dqkv = dilated_proj(hn, p["w_dil"], layer=i)
    cqn = rmsnorm(lat, p["g_cq"][i], BF16, col_block=0, width=Q_LORA)
    ckvn = rmsnorm(lat, p["g_ckv"][i], BF16, col_block=1, width=KV_LORA)
    q_raw = matmul([cqn], [(p["w_uq"], 0)], F32, layer=i)
    kv = matmul([ckvn], [(p["w_ukv"], 0)], BF16, layer=i)
    q_full, k_full = mla_prep(q_raw, lat, kv, tables, seq=seq)
    o_c = flash_attention(q_full, k_full, kv, batch=batch, seq=seq, dqk=MLA_QK,
                          q_col=lambda hd: hd, k_col=lambda hd: hd, v_col=lambda hd: 2 * hd + 1,
                          scale=MLA_SCALE_DIM ** -0.5)
    branches = [dilated_branch(x, batch=batch, seq=seq, rate=r) for x, r in zip(dqkv, DIL_RATES)]
    o_d = dilated_merge([o for o, _ in branches], [l for _, l in branches])
    h = matmul([o_c, o_d], [(p["w_o"], 0), (p["w_o"], 1)], F32, res=h, layer=i)
    return moe_ffn(h, p["norm_ffn"][i], p["w_router"][i], p["w_exp_gate"], p["w_exp_up"],
                   p["w_exp_down"], g_next, norm_dtype, layer=i)


def kernel(x, even_norm_mix, even_w_in, even_b_f, even_w_o, even_norm_ffn, even_w_gate, even_w_up,
           even_w_down, odd_norm_mix, odd_w_in, odd_g_cq, odd_g_ckv, odd_w_uq, odd_w_ukv, odd_w_o,
           odd_norm_ffn, odd_w_router, odd_w_exp_gate, odd_w_exp_up, odd_w_exp_down, final_norm):
    batch, seq, d = x.shape
    assert seq % (max(DIL_RATES) * DIL_BLOCK) == 0
    even = prep_even(dict(norm_mix=even_norm_mix, w_in=even_w_in, b_f=even_b_f, w_o=even_w_o,
                          norm_ffn=even_norm_ffn, w_gate=even_w_gate, w_up=even_w_up,
                          w_down=even_w_down))
    odd = prep_odd(dict(norm_mix=odd_norm_mix, w_in=odd_w_in, g_cq=odd_g_cq, g_ckv=odd_g_ckv,
                        w_uq=odd_w_uq, w_ukv=odd_w_ukv, w_o=odd_w_o, norm_ffn=odd_norm_ffn,
                        w_router=odd_w_router, w_exp_gate=odd_w_exp_gate, w_exp_up=odd_w_exp_up,
                        w_exp_down=odd_w_exp_down))
    depth = even_w_in.shape[0] + odd_w_in.shape[0]
    tables = rope_tables(seq)
    assert depth % 2 == 0, "the final norm is fused into the last odd layer"
    h = x.reshape(batch * seq, d)
    hn = rmsnorm(h, even_norm_mix[0], BF16)
    for layer in range(depth):
        i = layer // 2
        if layer % 2 == 0:
            h, hn = even_layer(h, hn, even, i, odd_norm_mix[i], batch=batch, seq=seq)
        elif layer + 1 < depth:
            h, hn = odd_layer(h, hn, odd, i, tables, even_norm_mix[i + 1], BF16, batch=batch, seq=seq)
        else:
            h, hn = odd_layer(h, hn, odd, i, tables, final_norm, x.dtype, batch=batch, seq=seq)
    return hn.reshape(batch, seq, d)
```

```python
import functools

import jax
import jax.numpy as jnp
from jax import lax
from jax.experimental import pallas as pl
from jax.experimental.pallas import tpu as pltpu

F32 = jnp.float32
BF16 = jnp.bfloat16

LANE = 128
SUBLANE = 8
HEAD_DIM = 128
N_HEADS = 8
Q_LORA = 512
KV_LORA = 512
ROPE_DIM = 64
MLA_QK = 256
MLA_SCALE_DIM = 192
ROPE_THETA = 10000.0
DIL_RATES = (1, 4, 16)
DIL_BLOCK = 128
N_EXPERTS = 8
NORM_EPS = 1e-6
NEG_INF = float("-inf")
LOG2E = 1.4426950408889634

ROW_TILE = 512
MM_TILE = 1024
ATT_TQ = 512
ATT_TK = 512
SB_TK = 256
FFN_TM = 512
FFN_TF = 512
MOE_TM = 1024
MOE_SUB = 512
MOE_TF = 256
DMA_ROWS = 256
DMA_UNROLL = 4


def _rms(x, g):
    ms = jnp.mean(x * x, axis=-1, keepdims=True)
    return x * lax.rsqrt(ms + NORM_EPS) * g


def _log_sigmoid(x):
    return -(jnp.maximum(-x, 0.0) + jnp.log1p(jnp.exp(-jnp.abs(x))))


def _rmsnorm_body(x_ref, g_ref, o_ref):
    o_ref[...] = _rms(x_ref[...].astype(F32), g_ref[...]).astype(o_ref.dtype)


def rmsnorm(x, g, out_dtype, *, col_block=0, width=None):
    m = x.shape[0]
    width = x.shape[1] if width is None else width
    tm = min(ROW_TILE, m)
    return pl.pallas_call(
        _rmsnorm_body,
        grid=(m // tm,),
        in_specs=[pl.BlockSpec((tm, width), lambda i: (i, col_block)),
                  pl.BlockSpec((1, width), lambda i: (0, 0))],
        out_specs=pl.BlockSpec((tm, width), lambda i: (i, 0)),
        out_shape=jax.ShapeDtypeStruct((m, width), out_dtype),
        name="rmsnorm",
    )(x, g.reshape(1, width).astype(F32))


def _matmul_body(*refs, n_in, has_res):
    o_ref = refs[-1]
    acc = jnp.dot(refs[0][...], refs[n_in][...], preferred_element_type=F32)
    for i in range(1, n_in):
        acc = acc + jnp.dot(refs[i][...], refs[n_in + i][...], preferred_element_type=F32)
    if has_res:
        acc = refs[2 * n_in][...] + acc
    o_ref[...] = acc.astype(o_ref.dtype)


def matmul(a_list, w_list, out_dtype, res=None, *, layer=0, n=None):
    m = a_list[0].shape[0]
    n = w_list[0][0].shape[2] if n is None else n
    tm = min(MM_TILE, m)
    tn = MM_TILE if n % MM_TILE == 0 else n

    def w_spec(a, row_block):
        return pl.BlockSpec((None, a.shape[1], tn), lambda i, j: (layer, row_block, j))

    in_specs = [pl.BlockSpec((tm, a.shape[1]), lambda i, j: (i, 0)) for a in a_list]
    in_specs += [w_spec(a, rb) for a, (_, rb) in zip(a_list, w_list)]
    args = list(a_list) + [w for w, _ in w_list]
    if res is not None:
        in_specs.append(pl.BlockSpec((tm, tn), lambda i, j: (i, j)))
        args.append(res)
    return pl.pallas_call(
        functools.partial(_matmul_body, n_in=len(a_list), has_res=res is not None),
        grid=(m // tm, n // tn),
        in_specs=in_specs,
        out_specs=pl.BlockSpec((tm, tn), lambda i, j: (i, j)),
        out_shape=jax.ShapeDtypeStruct((m, n), out_dtype),
        name="matmul",
    )(*args)


def _out_proj_norm_body(*refs, n_in):
    res_ref, g_ref, h_ref, hn_ref = refs[2 * n_in:]
    acc = jnp.dot(refs[0][...], refs[n_in][...], preferred_element_type=F32)
    for i in range(1, n_in):
        acc = acc + jnp.dot(refs[i][...], refs[n_in + i][...], preferred_element_type=F32)
    h = res_ref[...] + acc
    h_ref[...] = h
    hn_ref[...] = _rms(h, g_ref[...]).astype(hn_ref.dtype)


def out_proj_norm(a_list, w, res, g, *, layer):
    m, d = res.shape
    tm = min(ROW_TILE, m)

    def w_spec(a, row_block):
        return pl.BlockSpec((None, a.shape[1], d), lambda i: (layer, row_block, 0))

    row = pl.BlockSpec((tm, d), lambda i: (i, 0))
    return pl.pallas_call(
        functools.partial(_out_proj_norm_body, n_in=len(a_list)),
        grid=(m // tm,),
        in_specs=[pl.BlockSpec((tm, a.shape[1]), lambda i: (i, 0)) for a in a_list]
        + [w_spec(a, rb) for rb, a in enumerate(a_list)]
        + [row, pl.BlockSpec((1, d), lambda i: (0, 0))],
        out_specs=[row, row],
        out_shape=[jax.ShapeDtypeStruct((m, d), F32), jax.ShapeDtypeStruct((m, d), BF16)],
        name="out_proj_norm",
    )(*a_list, *([w] * len(a_list)), res, g.reshape(1, d).astype(F32))


def _decay_body(fg_ref, b_ref, c_ref, *, seq):
    bias = b_ref[...]
    row = lax.broadcasted_iota(jnp.int32, (SUBLANE, LANE), 0)

    def body(i, carry):
        st = pl.multiple_of(i * SUBLANE, SUBLANE)
        x = _log_sigmoid(fg_ref[pl.ds(st, SUBLANE), :] + bias)
        for s in (1, 2, 4):
            x = x + jnp.where(row >= s, pltpu.roll(x, s, 0), 0.0)
        x = x + carry
        c_ref[pl.ds(st, SUBLANE), :] = x
        return jnp.broadcast_to(x[SUBLANE - 1:SUBLANE, :], (SUBLANE, LANE))

    lax.fori_loop(0, seq // SUBLANE, body, jnp.zeros((SUBLANE, LANE), F32), unroll=8)


def decay_cumsum(fg, b_f, batch, seq):
    bias = jnp.zeros((1, LANE), F32).at[0, :b_f.shape[0]].set(b_f.astype(F32))
    return pl.pallas_call(
        functools.partial(_decay_body, seq=seq),
        grid=(batch,),
        in_specs=[pl.BlockSpec((seq, LANE), lambda b: (b, 0)),
                  pl.BlockSpec((1, LANE), lambda b: (0, 0))],
        out_specs=pl.BlockSpec((seq, LANE), lambda b: (b, 0)),
        out_shape=jax.ShapeDtypeStruct((batch * seq, LANE), F32),
        name="decay_cumsum",
    )(fg, bias)


def _flash_body(*refs, tq, tk, scale, has_decay):
    if has_decay:
        q_ref, k_ref, v_ref, c_ref, crow_ref, o_ref = refs
    else:
        q_ref, k_ref, v_ref, o_ref = refs
    h = pl.program_id(1)
    qi = pl.program_id(2)
    q = q_ref[...]
    to_log2 = scale * LOG2E
    if has_decay:
        cq = crow_ref[:, pl.ds(pl.multiple_of(qi * tq, tq), tq)] * LOG2E

    def span(kb, n, carry, masked):
        m, l, acc = carry
        start = pl.multiple_of(kb * tk, tk)
        k = k_ref[pl.ds(start, n * tk), :]
        t_all = lax.dot_general(k, q, (((1,), (1,)), ((), ())), preferred_element_type=F32) * to_log2
        if has_decay:
            cblk = c_ref[pl.ds(start, n * tk), :]
            lane = lax.broadcasted_iota(jnp.int32, cblk.shape, 1)
            ck = jnp.sum(jnp.where(lane == h, cblk, 0.0), axis=-1, keepdims=True)
            t_all = t_all - ck * LOG2E
        for j in range(n):
            t = t_all[j * tk:(j + 1) * tk, :]
            if masked:
                key = lax.broadcasted_iota(jnp.int32, (tk, tq), 0)
                qry = lax.broadcasted_iota(jnp.int32, (tk, tq), 1)
                t = jnp.where(key <= qry, t, NEG_INF)
            top = jnp.max(t, axis=0, keepdims=True)
            if has_decay:
                top = top + cq
            m_new = jnp.maximum(m, top)
            shift = m_new - cq if has_decay else m_new
            alpha = jnp.exp2(m - m_new)
            p = jnp.exp2(t - shift)
            l = alpha * l + jnp.sum(p, axis=0, keepdims=True)
            v = v_ref[pl.ds(pl.multiple_of(start + j * tk, tk), tk), :]
            pv = lax.dot_general(v, p.astype(BF16), (((0,), (0,)), ((), ())),
                                 preferred_element_type=F32)
            acc = alpha * acc + pv
            m = m_new
        return m, l, acc

    init = (jnp.full((1, tq), NEG_INF, F32), jnp.zeros((1, tq), F32),
            jnp.zeros((v_ref.shape[1], tq), F32))
    carry = lax.fori_loop(0, qi // 4, lambda i, c: span(4 * i, 4, c, False), init)
    carry = lax.cond(qi % 4 >= 2, lambda c: span((qi // 4) * 4, 2, c, False), lambda c: c, carry)
    carry = lax.cond(qi % 2 == 1, lambda c: span(qi - 1, 1, c, False), lambda c: c, carry)
    m, l, acc = span(qi, 1, carry, True)
    o_ref[...] = (acc / l).T.astype(o_ref.dtype)


def flash_attention(q_arr, k_arr, v_arr, *, batch, seq, dqk, q_col, k_col, v_col, scale,
                    decay=None):
    tq = tk = min(ATT_TQ, seq)
    nq = seq // tq
    in_specs = [pl.BlockSpec((tq, dqk), lambda b, h, i: (b * nq + i, q_col(h))),
                pl.BlockSpec((seq, dqk), lambda b, h, i: (b, k_col(h))),
                pl.BlockSpec((seq, HEAD_DIM), lambda b, h, i: (b, v_col(h)))]
    args = [q_arr, k_arr, v_arr]
    if decay is not None:
        in_specs += [pl.BlockSpec((seq, LANE), lambda b, h, i: (b, 0)),
                     pl.BlockSpec((None, None, 1, seq), lambda b, h, i: (b, h, 0, 0))]
        args += list(decay)
    return pl.pallas_call(
        functools.partial(_flash_body, tq=tq, tk=tk, scale=scale, has_decay=decay is not None),
        grid=(batch, N_HEADS, nq),
        in_specs=in_specs,
        out_specs=pl.BlockSpec((tq, HEAD_DIM), lambda b, h, i: (b * nq + i, h)),
        out_shape=jax.ShapeDtypeStruct((batch * seq, N_HEADS * HEAD_DIM), BF16),
        name="flash_attention",
    )(*args)


def _sb_body(q_ref, k_ref, v_ref, o_ref, *, tq, tk, scale):
    qi = pl.program_id(2)
    q = q_ref[...]
    jrow = lax.broadcasted_iota(jnp.int32, (tk, tk), 0)
    scol = lax.broadcasted_iota(jnp.int32, (tk, tk), 1)
    later = jnp.where(jrow > scol, 1.0, 0.0).astype(BF16)
    later2 = jnp.concatenate([later, later], axis=0)

    def span(q_block, n_q, carry, masked):
        run, acc = carry
        width = n_q * tq
        base = pl.multiple_of(q_block * tq, tq)
        k = k_ref[pl.ds(base, width), :]
        v = v_ref[pl.ds(base, width), :]
        z = lax.dot_general(q, k, (((1,), (1,)), ((), ())), preferred_element_type=F32) * (scale * LOG2E)
        sp = jnp.maximum(z, 0.0) + jnp.log2(1.0 + jnp.exp2(-jnp.abs(z)))
        l1m = -sp
        if masked:
            strict = (lax.broadcasted_iota(jnp.int32, (tq, width), 1)
                      < lax.broadcasted_iota(jnp.int32, (tq, width), 0))
            l1m = jnp.where(strict, l1m, 0.0)
        hi = l1m.astype(BF16)
        lo = (l1m - hi.astype(F32)).astype(BF16)
        blocks = [slice(j * tk, (j + 1) * tk) for j in range(width // tk)]
        after = [jnp.dot(jnp.concatenate([hi[:, b], lo[:, b]], axis=1), later2,
                         preferred_element_type=F32) for b in blocks]
        sums = [jnp.sum(l1m[:, b], axis=-1, keepdims=True) for b in blocks]
        logw = []
        for j in reversed(range(len(blocks))):
            logw.append((z[:, blocks[j]] - sp[:, blocks[j]]) + after[j] + run)
            run = run + sums[j]
        a = jnp.exp2(jnp.concatenate(logw[::-1], axis=1))
        if masked:
            a = jnp.where(strict, a, 0.0)
        acc = acc + jnp.dot(a.astype(BF16), v, preferred_element_type=F32)
        return run, acc

    carry = (jnp.zeros((tq, 1), F32), jnp.zeros((tq, HEAD_DIM), F32))
    carry = span(qi, 1, carry, True)
    carry = lax.cond(qi % 2 == 1, lambda c: span(qi - 1, 1, c, False), lambda c: c, carry)
    n_quads = qi // 4
    carry = lax.cond(qi % 4 >= 2, lambda c: span(4 * n_quads, 2, c, False), lambda c: c, carry)
    _, acc = lax.fori_loop(0, n_quads, lambda i, c: span(4 * (n_quads - 1 - i), 4, c, False), carry)
    o_ref[...] = acc.astype(o_ref.dtype)


def stick_breaking_attention(qkv, *, batch, seq, q_col, k_col, v_col):
    tq = min(ATT_TQ, seq)
    tk = min(SB_TK, tq)
    nq = seq // tq
    return pl.pallas_call(
        functools.partial(_sb_body, tq=tq, tk=tk, scale=HEAD_DIM ** -0.5),
        grid=(batch, N_HEADS, nq),
        in_specs=[pl.BlockSpec((tq, HEAD_DIM), lambda b, h, i: (b * nq + i, q_col(h))),
                  pl.BlockSpec((seq, HEAD_DIM), lambda b, h, i: (b, k_col(h))),
                  pl.BlockSpec((seq, HEAD_DIM), lambda b, h, i: (b, v_col(h)))],
        out_specs=pl.BlockSpec((tq, HEAD_DIM), lambda b, h, i: (b * nq + i, h)),
        out_shape=jax.ShapeDtypeStruct((batch * seq, N_HEADS * HEAD_DIM), BF16),
        name="stick_breaking",
    )(qkv, qkv, qkv)


def _mla_prep_body(q_ref, kr_ref, kv_ref, cos_ref, sin_lo_ref, sin_hi_ref, qf_ref, kf_ref):
    cos, sin_lo, sin_hi = cos_ref[...], sin_lo_ref[...], sin_hi_ref[...]

    def rope(x):
        return x * cos + pltpu.roll(x, LANE - ROPE_DIM // 2, 1) * sin_lo \
            + pltpu.roll(x, ROPE_DIM // 2, 1) * sin_hi

    k_pe = rope(kr_ref[...]).astype(BF16)
    for h in range(N_HEADS):
        lo = h * MLA_QK
        qf_ref[:, lo:lo + HEAD_DIM] = q_ref[:, lo:lo + HEAD_DIM].astype(BF16)
        qf_ref[:, lo + HEAD_DIM:lo + MLA_QK] = rope(q_ref[:, lo + HEAD_DIM:lo + MLA_QK]).astype(BF16)
        kf_ref[:, lo:lo + HEAD_DIM] = kv_ref[:, lo:lo + HEAD_DIM]
        kf_ref[:, lo + HEAD_DIM:lo + MLA_QK] = k_pe


def mla_prep(q_raw, lat, kv, tables, *, seq):
    m = q_raw.shape[0]
    tm = min(ROW_TILE, seq)
    npos = seq // tm
    wide = N_HEADS * MLA_QK
    kr_block = (Q_LORA + KV_LORA) // LANE
    tab_spec = pl.BlockSpec((tm, LANE), lambda i: (i % npos, 0))
    return pl.pallas_call(
        _mla_prep_body,
        grid=(m // tm,),
        in_specs=[pl.BlockSpec((tm, wide), lambda i: (i, 0)),
                  pl.BlockSpec((tm, LANE), lambda i: (i, kr_block)),
                  pl.BlockSpec((tm, wide), lambda i: (i, 0)),
                  tab_spec, tab_spec, tab_spec],
        out_specs=[pl.BlockSpec((tm, wide), lambda i: (i, 0)),
                   pl.BlockSpec((tm, wide), lambda i: (i, 0))],
        out_shape=[jax.ShapeDtypeStruct((m, wide), BF16),
                   jax.ShapeDtypeStruct((m, wide), BF16)],
        name="mla_prep",
    )(q_raw, lat, kv, *tables)


def rope_tables(seq):
    half = ROPE_DIM // 2
    inv_freq = ROPE_THETA ** (-jnp.arange(half, dtype=F32) / half)
    ang = jnp.arange(seq, dtype=F32)[:, None] * inv_freq[None, :]
    cos, sin = jnp.cos(ang), jnp.sin(ang)
    zeros = jnp.zeros((seq, half), F32)
    pad = jnp.zeros((seq, LANE - ROPE_DIM), F32)
    cos_t = jnp.concatenate([cos, cos, pad], axis=1)
    sin_lo = jnp.concatenate([-sin, zeros, pad], axis=1)
    sin_hi = jnp.concatenate([zeros, sin, pad], axis=1)
    return cos_t, sin_lo, sin_hi


def _dilated_body(q_ref, kp_ref, kc_ref, vp_ref, vc_ref, o_ref, lse_ref, *, rate, scale):
    ib = pl.program_id(2)
    n = DIL_BLOCK
    qi = lax.broadcasted_iota(jnp.int32, (n, 2 * n), 0)
    ki = lax.broadcasted_iota(jnp.int32, (n, 2 * n), 1)
    steps = n + qi - ki
    valid = (steps >= 0) & (steps <= n) & ((ki >= n) | (ib > 0))
    dist = (steps * rate).astype(F32)
    lane = lax.broadcasted_iota(jnp.int32, (n, LANE), 1)
    lse_all = jnp.zeros((n, LANE), F32)
    for h in range(N_HEADS):
        cols = slice(h * HEAD_DIM, (h + 1) * HEAD_DIM)
        q = q_ref[:, cols]
        k = jnp.concatenate([kp_ref[:, cols], kc_ref[:, cols]], axis=0)
        v = jnp.concatenate([vp_ref[:, cols], vc_ref[:, cols]], axis=0)
        s = lax.dot_general(q, k, (((1,), (1,)), ((), ())), preferred_element_type=F32) * scale
        s = jnp.where(valid, s - (2.0 ** -(h + 1)) * dist, NEG_INF)
        m = jnp.max(s, axis=-1, keepdims=True)
        p = jnp.exp(s - m)
        l = jnp.sum(p, axis=-1, keepdims=True)
        o = jnp.dot(p.astype(BF16), v, preferred_element_type=F32) / l
        o_ref[:, cols] = o.astype(o_ref.dtype)
        lse_all = jnp.where(lane == h, m + jnp.log(l), lse_all)
    lse_ref[...] = lse_all


def _dilated_proj_body(a_ref, w_ref, *refs):
    out_refs, acc_ref = refs[:-1], refs[-1]
    acc = jnp.dot(a_ref[...], w_ref[...], preferred_element_type=F32)
    n_slabs, tm, _ = acc_ref.shape
    for j in range(n_slabs):
        acc_ref[j] = acc[:, j * LANE:(j + 1) * LANE]
    for rate, o_ref in zip(DIL_RATES, out_refs):
        if rate == 1:
            o_ref[...] = acc.astype(o_ref.dtype)
            continue
        for c in range(rate):
            for j in range(n_slabs):
                o_ref[c, :, j * LANE:(j + 1) * LANE] = (
                    acc_ref[j, pl.ds(c, tm // rate, stride=rate), :].astype(o_ref.dtype))


def dilated_proj(a, w, *, layer):
    m, k = a.shape
    n = w.shape[2]
    tm, tn = min(MM_TILE, m), MM_TILE

    def out_spec(rate):
        if rate == 1:
            return pl.BlockSpec((tm, tn), lambda i, j: (i, j))
        return pl.BlockSpec((rate, tm // rate, tn), lambda i, j: (0, i, j))

    def out_shape(rate):
        return jax.ShapeDtypeStruct((m, n) if rate == 1 else (rate, m // rate, n), BF16)

    return pl.pallas_call(
        _dilated_proj_body,
        grid=(m // tm, n // tn),
        in_specs=[pl.BlockSpec((tm, k), lambda i, j: (i, 0)),
                  pl.BlockSpec((None, k, tn), lambda i, j: (layer, 0, j))],
        out_specs=[out_spec(r) for r in DIL_RATES],
        out_shape=[out_shape(r) for r in DIL_RATES],
        scratch_shapes=[pltpu.VMEM((tn // LANE, tm, LANE), F32)],
        name="dilated_proj",
    )(a, w)


def dilated_branch(dqkv, *, batch, seq, rate):
    wide = N_HEADS * HEAD_DIM
    n_rows = batch * seq
    nb = seq // rate // DIL_BLOCK
    rows = dqkv.reshape(n_rows, 3 * wide)

    def row_block(b, c, i):
        return (c * batch + b) * nb + i

    def spec(which, prev):
        def index(b, c, i):
            return (row_block(b, c, jnp.maximum(i - 1, 0) if prev else i), which)
        return pl.BlockSpec((DIL_BLOCK, wide), index)

    o, lse = pl.pallas_call(
        functools.partial(_dilated_body, rate=rate, scale=HEAD_DIM ** -0.5),
        grid=(batch, rate, nb),
        in_specs=[spec(0, False), spec(1, True), spec(1, False), spec(2, True), spec(2, False)],
        out_specs=[pl.BlockSpec((DIL_BLOCK, wide), lambda b, c, i: (row_block(b, c, i), 0)),
                   pl.BlockSpec((DIL_BLOCK, LANE), lambda b, c, i: (row_block(b, c, i), 0))],
        out_shape=[jax.ShapeDtypeStruct((n_rows, wide), BF16),
                   jax.ShapeDtypeStruct((n_rows, LANE), F32)],
        name="dilated_branch",
    )(rows, rows, rows, rows, rows)
    if rate == 1:
        return o, lse
    return o.reshape(rate, n_rows // rate, wide), lse.reshape(rate, n_rows // rate, LANE)


def _dilated_merge_body(*refs):
    n = len(DIL_RATES)
    o_refs, l_refs, out_ref = refs[:n], refs[n:2 * n], refs[2 * n]
    scratch = refs[2 * n + 1:]
    outs, lses = [], []
    for g, rate in enumerate(DIL_RATES):
        if rate == 1:
            outs.append(o_refs[g][...].astype(F32))
            lses.append(l_refs[g][...])
            continue
        o_nat, l_nat = scratch[2 * (g - 1)], scratch[2 * (g - 1) + 1]
        rows = l_nat.shape[0] // rate
        for c in range(rate):
            for h in range(N_HEADS):
                o_nat[h, pl.ds(c, rows, stride=rate), :] = (
                    o_refs[g][c, :, h * HEAD_DIM:(h + 1) * HEAD_DIM].astype(F32))
            l_nat[pl.ds(c, rows, stride=rate), :] = l_refs[g][c]
        outs.append(o_nat)
        lses.append(l_nat[...])
    top = functools.reduce(jnp.maximum, lses)
    es = [jnp.exp(x - top) for x in lses]
    den = functools.reduce(lambda a, b: a + b, es)
    ws = [e / den for e in es]
    for h in range(N_HEADS):
        cols = slice(h * HEAD_DIM, (h + 1) * HEAD_DIM)
        acc = ws[0][:, h:h + 1] * outs[0][:, cols]
        for g in range(1, n):
            acc = acc + ws[g][:, h:h + 1] * outs[g][h]
        out_ref[:, cols] = acc.astype(out_ref.dtype)


def dilated_merge(outs, lses):
    m = outs[0].shape[0]
    wide = N_HEADS * HEAD_DIM
    tm = min(ROW_TILE, m)

    def spec(rate, width):
        if rate == 1:
            return pl.BlockSpec((tm, width), lambda i: (i, 0))
        return pl.BlockSpec((rate, tm // rate, width), lambda i: (0, i, 0))

    scratch = []
    for rate in DIL_RATES[1:]:
        scratch += [pltpu.VMEM((N_HEADS, tm, HEAD_DIM), F32), pltpu.VMEM((tm, LANE), F32)]
    return pl.pallas_call(
        _dilated_merge_body,
        grid=(m // tm,),
        in_specs=[spec(r, wide) for r in DIL_RATES] + [spec(r, LANE) for r in DIL_RATES],
        out_specs=pl.BlockSpec((tm, wide), lambda i: (i, 0)),
        out_shape=jax.ShapeDtypeStruct((m, wide), BF16),
        scratch_shapes=scratch,
        name="dilated_merge",
    )(*outs, *lses)


def _swiglu_partial(x, wg, wu, wd):
    g = jnp.dot(x, wg, preferred_element_type=F32)
    u = jnp.dot(x, wu, preferred_element_type=F32)
    mid = (g * (1.0 / (1.0 + jnp.exp(-g))) * u).astype(BF16)
    return jnp.dot(mid, wd, preferred_element_type=F32)


def _dense_ffn_body(x_ref, wg_ref, wu_ref, wd_ref, res_ref, g_ref, o_ref, on_ref):
    f = pl.program_id(1)

    @pl.when(f == 0)
    def _():
        o_ref[...] = res_ref[...]

    o_ref[...] += _swiglu_partial(x_ref[...], wg_ref[...], wu_ref[...], wd_ref[...])

    @pl.when(f == pl.num_programs(1) - 1)
    def _():
        on_ref[...] = _rms(o_ref[...], g_ref[...]).astype(on_ref.dtype)


def dense_ffn(x, w_gate, w_up, w_down, res, g_next, *, layer):
    m, d = x.shape
    ff = w_gate.shape[2]
    tm, tf = min(FFN_TM, m), min(FFN_TF, ff)
    row = pl.BlockSpec((tm, d), lambda i, f: (i, 0))
    return pl.pallas_call(
        _dense_ffn_body,
        grid=(m // tm, ff // tf),
        in_specs=[row,
                  pl.BlockSpec((None, d, tf), lambda i, f: (layer, 0, f)),
                  pl.BlockSpec((None, d, tf), lambda i, f: (layer, 0, f)),
                  pl.BlockSpec((None, tf, d), lambda i, f: (layer, f, 0)),
                  row,
                  pl.BlockSpec((1, d), lambda i, f: (0, 0))],
        out_specs=[row, row],
        out_shape=[jax.ShapeDtypeStruct((m, d), F32), jax.ShapeDtypeStruct((m, d), BF16)],
        name="dense_ffn",
    )(x, w_gate, w_up, w_down, res, g_next.reshape(1, d).astype(F32))


def _pack_bf16_pairs(lo, hi):
    lo_bits = lax.bitcast_convert_type(lo.astype(BF16).astype(F32), jnp.uint32)
    hi_bits = lax.bitcast_convert_type(hi.astype(BF16).astype(F32), jnp.uint32)
    return (lo_bits >> 16) | (hi_bits & jnp.uint32(0xFFFF0000))


def _unpack_bf16_pairs(packed):
    lo = lax.bitcast_convert_type(packed << 16, F32).astype(BF16)
    hi = lax.bitcast_convert_type(packed & jnp.uint32(0xFFFF0000), F32).astype(BF16)
    return lo, hi


def _grouped_ffn_body(meta_ref, x_ref, wg_ref, wu_ref, wd_ref, o_ref, xb_ref, *, n_tiles, sub):
    i = pl.program_id(0)
    f = pl.program_id(1)
    in_use = i < meta_ref[n_tiles]
    rows = meta_ref[n_tiles + 1 + i]
    half = x_ref.shape[1]

    @pl.when(f == 0)
    def _():
        o_ref[...] = jnp.zeros_like(o_ref)

    @pl.when(in_use & (f == 0))
    def _():
        lo, hi = _unpack_bf16_pairs(x_ref[...])
        xb_ref[:, :half] = lo
        xb_ref[:, half:] = hi

    tm = x_ref.shape[0]
    half_sub = sub // 2

    def run_blocks(blocks):
        wg, wu, wd = (w[...].astype(BF16) for w in (wg_ref, wu_ref, wd_ref))
        for start, size in blocks:
            sl = pl.ds(start, size)
            o_ref[sl, :] += _swiglu_partial(xb_ref[sl, :], wg, wu, wd)

    for n_half in range(1, tm // half_sub + 1):
        full, tail = divmod(n_half, 2)
        blocks = [(b * sub, sub) for b in range(full)] + [(full * sub, half_sub)] * tail
        lo, hi = (n_half - 1) * half_sub, n_half * half_sub
        pl.when(in_use & (rows > lo) & (rows <= hi))(functools.partial(run_blocks, blocks))


def grouped_ffn(xs, meta, w_gate, w_up, w_down, *, layer, tm):
    ns, half = xs.shape
    d = 2 * half
    n_tiles = ns // tm
    ff = w_gate.shape[3]
    tf = min(MOE_TF, ff)
    nf = ff // tf

    def tile(i, meta):
        return jnp.maximum(jnp.minimum(i, meta[n_tiles] - 1), 0)

    def expert(i, meta):
        return meta[tile(i, meta)]

    def chunk(i, f, meta):
        return jnp.where(i < meta[n_tiles], f, nf - 1)

    grid_spec = pltpu.PrefetchScalarGridSpec(
        num_scalar_prefetch=1,
        grid=(n_tiles, nf),
        in_specs=[pl.BlockSpec((tm, half), lambda i, f, meta: (tile(i, meta), 0)),
                  pl.BlockSpec((None, None, d, tf),
                               lambda i, f, meta: (layer, expert(i, meta), 0, chunk(i, f, meta))),
                  pl.BlockSpec((None, None, d, tf),
                               lambda i, f, meta: (layer, expert(i, meta), 0, chunk(i, f, meta))),
                  pl.BlockSpec((None, None, tf, d),
                               lambda i, f, meta: (layer, expert(i, meta), chunk(i, f, meta), 0))],
        out_specs=pl.BlockSpec((tm, d), lambda i, f, meta: (i, 0)),
        scratch_shapes=[pltpu.VMEM((tm, d), BF16)],
    )
    return pl.pallas_call(
        functools.partial(_grouped_ffn_body, n_tiles=n_tiles, sub=min(MOE_SUB, tm)),
        grid_spec=grid_spec,
        out_shape=jax.ShapeDtypeStruct((ns, d), F32),
        name="grouped_ffn",
    )(meta, xs, w_gate, w_up, w_down)


ROUTE_I1, ROUTE_I2, ROUTE_G1, ROUTE_G2, ROUTE_P1, ROUTE_P2 = range(6)


def _router_body(h_ref, g_ref, wr_ref, route_ref, cnt_ref, seen_ref, *, tr):
    @pl.when(pl.program_id(0) == 0)
    def _():
        seen_ref[...] = jnp.zeros_like(seen_ref)

    y = _rms(h_ref[...], g_ref[...])
    y_hi = y.astype(BF16)
    y_lo = (y - y_hi.astype(F32)).astype(BF16)
    w = wr_ref[...]
    w_hi = w.astype(BF16)
    w_lo = (w - w_hi.astype(F32)).astype(BF16)

    def dot_nt(a, b):
        return lax.dot_general(a, b, (((1,), (1,)), ((), ())), preferred_element_type=F32)

    logits = dot_nt(w_hi, y_hi) + (dot_nt(w_hi, y_lo) + dot_nt(w_lo, y_hi))
    expert = lax.broadcasted_iota(jnp.int32, (N_EXPERTS, tr), 0)
    m1 = jnp.max(logits, axis=0, keepdims=True)
    i1 = jnp.min(jnp.where(logits == m1, expert, N_EXPERTS), axis=0, keepdims=True)
    rest = jnp.where(expert == i1, NEG_INF, logits)
    m2 = jnp.max(rest, axis=0, keepdims=True)
    i2 = jnp.min(jnp.where(rest == m2, expert, N_EXPERTS), axis=0, keepdims=True)
    e = jnp.exp(m2 - m1)
    g1 = 1.0 / (1.0 + e)
    g2 = e / (1.0 + e)
    sel = jnp.where((expert == i1) | (expert == i2), 1.0, 0.0)
    r = lax.broadcasted_iota(jnp.int32, (tr, tr), 0)
    c = lax.broadcasted_iota(jnp.int32, (tr, tr), 1)
    earlier = jnp.where(r < c, 1.0, 0.0).astype(BF16)
    seen = seen_ref[...]
    rank = jnp.dot(sel.astype(BF16), earlier, preferred_element_type=F32) + seen[:, 0:1]
    p1 = jnp.sum(jnp.where(expert == i1, rank, 0.0), axis=0, keepdims=True)
    p2 = jnp.sum(jnp.where(expert == i2, rank, 0.0), axis=0, keepdims=True)
    seen = seen + jnp.sum(sel, axis=1, keepdims=True)
    seen_ref[...] = seen
    cnt_ref[...] = seen
    route = jnp.zeros((SUBLANE, tr), F32)
    row = lax.broadcasted_iota(jnp.int32, (SUBLANE, tr), 0)
    for idx, val in ((ROUTE_I1, i1.astype(F32)), (ROUTE_I2, i2.astype(F32)), (ROUTE_G1, g1),
                     (ROUTE_G2, g2), (ROUTE_P1, p1), (ROUTE_P2, p2)):
        route = jnp.where(row == idx, val, route)
    route_ref[...] = route


def router(h, g, w_router):
    assert N_EXPERTS == SUBLANE
    m, d = h.shape
    tr = min(ROW_TILE, m)
    return pl.pallas_call(
        functools.partial(_router_body, tr=tr),
        grid=(m // tr,),
        in_specs=[pl.BlockSpec((tr, d), lambda i: (i, 0)),
                  pl.BlockSpec((1, d), lambda i: (0, 0)),
                  pl.BlockSpec((N_EXPERTS, d), lambda i: (0, 0))],
        out_specs=[pl.BlockSpec((SUBLANE, tr), lambda i: (0, i)),
                   pl.BlockSpec((N_EXPERTS, LANE), lambda i: (0, 0))],
        out_shape=[jax.ShapeDtypeStruct((SUBLANE, m), F32),
                   jax.ShapeDtypeStruct((N_EXPERTS, LANE), F32)],
        scratch_shapes=[pltpu.VMEM((N_EXPERTS, LANE), F32)],
        name="router",
    )(h, g.reshape(1, d).astype(F32), w_router.astype(F32).T)


def _row_copies(copies, step, slot, tr, wait):
    def body(r, _):
        for queue, cp in enumerate(copies(step, slot, r)):
            if wait:
                cp.wait()
            else:
                cp.start(priority=queue)
        return 0

    lax.fori_loop(0, tr, body, 0, unroll=DMA_UNROLL)


def _scatter_body(s1_ref, s2_ref, h_ref, g_ref, xs_in_ref, xs_ref, buf_ref, sems, *, tr):
    del xs_in_ref
    i = pl.program_id(0)
    slot = i % 2
    y = _rms(h_ref[...], g_ref[...])
    half = y.shape[1] // 2
    buf_ref[slot] = _pack_bf16_pairs(y[:, :half], y[:, half:])

    def copies(step, slot, r):
        src = buf_ref.at[slot, pl.ds(r, 1)]
        t = step * tr + r
        return (pltpu.make_async_copy(src, xs_ref.at[pl.ds(s1_ref[t], 1)], sems.at[slot]),
                pltpu.make_async_copy(src, xs_ref.at[pl.ds(s2_ref[t], 1)], sems.at[slot]))

    _row_copies(copies, i, slot, tr, wait=False)

    @pl.when(i > 0)
    def _():
        _row_copies(copies, i - 1, 1 - slot, tr, wait=True)

    @pl.when(i == pl.num_programs(0) - 1)
    def _():
        _row_copies(copies, i, slot, tr, wait=True)


def scatter_rows(h, g, slot1, slot2, n_slots):
    m, d = h.shape
    tr = min(DMA_ROWS, m)
    grid_spec = pltpu.PrefetchScalarGridSpec(
        num_scalar_prefetch=2,
        grid=(m // tr,),
        in_specs=[pl.BlockSpec((tr, d), lambda i, s1, s2: (i, 0)),
                  pl.BlockSpec((1, d), lambda i, s1, s2: (0, 0)),
                  pl.BlockSpec(memory_space=pl.ANY)],
        out_specs=pl.BlockSpec(memory_space=pl.ANY),
        scratch_shapes=[pltpu.VMEM((2, tr, d // 2), jnp.uint32), pltpu.SemaphoreType.DMA((2,))],
    )
    return pl.pallas_call(
        functools.partial(_scatter_body, tr=tr),
        grid_spec=grid_spec,
        out_shape=jax.ShapeDtypeStruct((n_slots, d // 2), jnp.uint32),
        input_output_aliases={4: 0},
        compiler_params=pltpu.CompilerParams(dimension_semantics=("arbitrary",)),
        name="scatter_rows",
    )(slot1, slot2, h, g.reshape(1, d).astype(F32), jnp.zeros((n_slots, d // 2), jnp.uint32))


def _combine_body(s1_ref, s2_ref, h_ref, gate_ref, g_ref, y_ref, o_ref, on_ref, buf_ref, sems, *, tr):
    i = pl.program_id(0)
    slot = i % 2

    def copies(step, slot, r):
        t = step * tr + r
        return (pltpu.make_async_copy(y_ref.at[pl.ds(s1_ref[t], 1)], buf_ref.at[slot, 0, pl.ds(r, 1)],
                                      sems.at[slot]),
                pltpu.make_async_copy(y_ref.at[pl.ds(s2_ref[t], 1)], buf_ref.at[slot, 1, pl.ds(r, 1)],
                                      sems.at[slot]))

    @pl.when(i == 0)
    def _():
        _row_copies(copies, 0, 0, tr, wait=False)

    @pl.when(i + 1 < pl.num_programs(0))
    def _():
        _row_copies(copies, i + 1, 1 - slot, tr, wait=False)

    _row_copies(copies, i, slot, tr, wait=True)
    gates = gate_ref[...]
    h = h_ref[...] + (gates[:, 0:1] * buf_ref[slot, 0] + gates[:, 1:2] * buf_ref[slot, 1])
    o_ref[...] = h
    on_ref[...] = _rms(h, g_ref[...]).astype(on_ref.dtype)


def combine_rows(h, gates, y, slot1, slot2, g_next, norm_dtype):
    m, d = h.shape
    tr = min(DMA_ROWS, m)
    row = pl.BlockSpec((tr, d), lambda i, s1, s2: (i, 0))
    grid_spec = pltpu.PrefetchScalarGridSpec(
        num_scalar_prefetch=2,
        grid=(m // tr,),
        in_specs=[row,
                  pl.BlockSpec((tr, LANE), lambda i, s1, s2: (i, 0)),
                  pl.BlockSpec((1, d), lambda i, s1, s2: (0, 0)),
                  pl.BlockSpec(memory_space=pl.ANY)],
        out_specs=[row, row],
        scratch_shapes=[pltpu.VMEM((2, 2, tr, d), F32), pltpu.SemaphoreType.DMA((2,))],
    )
    return pl.pallas_call(
        functools.partial(_combine_body, tr=tr),
        grid_spec=grid_spec,
        out_shape=[jax.ShapeDtypeStruct((m, d), F32), jax.ShapeDtypeStruct((m, d), norm_dtype)],
        compiler_params=pltpu.CompilerParams(dimension_semantics=("arbitrary",)),
        name="combine_rows",
    )(slot1, slot2, h, gates, g_next.reshape(1, d).astype(F32), y)


def moe_ffn(h, g, w_router, w_gate, w_up, w_down, g_next, norm_dtype, *, layer):
    m, _ = h.shape
    tm = min(MOE_TM, m)
    n_tiles = -(-(2 * m) // tm) + N_EXPERTS
    route, counts = router(h, g, w_router)
    col = lambda j: route[j].astype(jnp.int32)
    cnt = counts[:, 0].astype(jnp.int32)
    tiles_per = (cnt + tm - 1) // tm
    tile_end = jnp.cumsum(tiles_per)
    tile_start = tile_end - tiles_per
    group_start = tile_start * tm
    slot1 = group_start[col(ROUTE_I1)] + col(ROUTE_P1)
    slot2 = group_start[col(ROUTE_I2)] + col(ROUTE_P2)
    tile_ids = jnp.arange(n_tiles)
    tile_expert = jnp.minimum(jnp.sum(tile_ids[:, None] >= tile_end[None, :], axis=1), N_EXPERTS - 1)
    tile_rows = jnp.clip(cnt[tile_expert] - (tile_ids - tile_start[tile_expert]) * tm, 0, tm)
    meta = jnp.concatenate([tile_expert, tile_end[-1:], tile_rows]).astype(jnp.int32)
    xs = scatter_rows(h, g, slot1, slot2, n_tiles * tm)
    ys = grouped_ffn(xs, meta, w_gate, w_up, w_down, layer=layer, tm=tm)
    gates = jnp.zeros((m, LANE), F32).at[:, :2].set(route[ROUTE_G1:ROUTE_G2 + 1].T)
    return combine_rows(h, gates, ys, slot1, slot2, g_next, norm_dtype)


def prep_even(p):
    wide = N_HEADS * HEAD_DIM
    w_in = p["w_in"]
    n_layers, d, _ = w_in.shape
    w_fg = jnp.zeros((n_layers, d, LANE), BF16).at[:, :, :N_HEADS].set(w_in[:, :, 6 * wide:].astype(BF16))
    return dict(p, w_qkv=w_in[:, :, :6 * wide].astype(BF16), w_fg=w_fg, w_o=p["w_o"].astype(BF16),
                w_gate=p["w_gate"].astype(BF16), w_up=p["w_up"].astype(BF16),
                w_down=p["w_down"].astype(BF16))


def even_layer(h, hn, p, i, g_next, *, batch, seq):
    qkv = matmul([hn], [(p["w_qkv"], 0)], BF16, layer=i)
    fg = matmul([hn], [(p["w_fg"], 0)], F32, layer=i)
    c = decay_cumsum(fg, p["b_f"][i], batch, seq)
    c_row = c[:, :N_HEADS].reshape(batch, seq, N_HEADS).transpose(0, 2, 1).reshape(batch, N_HEADS, 1, seq)
    o_a = flash_attention(qkv, qkv, qkv, batch=batch, seq=seq, dqk=HEAD_DIM,
                          q_col=lambda hd: hd, k_col=lambda hd: N_HEADS + hd,
                          v_col=lambda hd: 2 * N_HEADS + hd, scale=HEAD_DIM ** -0.5,
                          decay=(c, c_row))
    o_b = stick_breaking_attention(qkv, batch=batch, seq=seq, q_col=lambda hd: 3 * N_HEADS + hd,
                                   k_col=lambda hd: 4 * N_HEADS + hd,
                                   v_col=lambda hd: 5 * N_HEADS + hd)
    h, hn = out_proj_norm([o_a, o_b], p["w_o"], h, p["norm_ffn"][i], layer=i)
    return dense_ffn(hn, p["w_gate"], p["w_up"], p["w_down"], h, g_next, layer=i)


def prep_odd(p):
    w_in = p["w_in"]
    n_layers, d, _ = w_in.shape
    lat_w = Q_LORA + KV_LORA
    w_lat = jnp.zeros((n_layers, d, lat_w + LANE), BF16).at[:, :, :lat_w + ROPE_DIM].set(
        w_in[:, :, :lat_w + ROPE_DIM].astype(BF16))
    w_uq = jnp.zeros((n_layers, Q_LORA, N_HEADS, MLA_QK), BF16).at[:, :, :, :HEAD_DIM + ROPE_DIM].set(
        p["w_uq"].reshape(n_layers, Q_LORA, N_HEADS, HEAD_DIM + ROPE_DIM).astype(BF16))
    return dict(p, w_lat=w_lat, w_dil=w_in[:, :, lat_w + ROPE_DIM:].astype(BF16),
                w_uq=w_uq.reshape(n_layers, Q_LORA, N_HEADS * MLA_QK),
                w_ukv=p["w_ukv"].astype(BF16), w_o=p["w_o"].astype(BF16))


def odd_layer(h, hn, p, i, tables, g_next, norm_dtype, *, batch, seq):
    lat = matmul([hn], [(p["w_lat"], 0)], F32, layer=i)
    dqkv = dilated_proj(hn, p["w_dil"], layer=i)
    cqn = rmsnorm(lat, p["g_cq"][i], BF16, col_block=0, width=Q_LORA)
    ckvn = rmsnorm(lat, p["g_ckv"][i], BF16, col_block=1, width=KV_LORA)
    q_raw = matmul([cqn], [(p["w_uq"], 0)], F32, layer=i)
    kv = matmul([ckvn], [(p["w_ukv"], 0)], BF16, layer=i)
    q_full, k_full = mla_prep(q_raw, lat, kv, tables, seq=seq)
    o_c = flash_attention(q_full, k_full, kv, batch=batch, seq=seq, dqk=MLA_QK,
                          q_col=lambda hd: hd, k_col=lambda hd: hd, v_col=lambda hd: 2 * hd + 1,
                          scale=MLA_SCALE_DIM ** -0.5)
    branches = [dilated_branch(x, batch=batch, seq=seq, rate=r) for x, r in zip(dqkv, DIL_RATES)]
    o_d = dilated_merge([o for o, _ in branches], [l for _, l in branches])
    h = matmul([o_c, o_d], [(p["w_o"], 0), (p["w_o"], 1)], F32, res=h, layer=i)
    return moe_ffn(h, p["norm_ffn"][i], p["w_router"][i], p["w_exp_gate"], p["w_exp_up"],
                   p["w_exp_down"], g_next, norm_dtype, layer=i)


def kernel(x, even_norm_mix, even_w_in, even_b_f, even_w_o, even_norm_ffn, even_w_gate, even_w_up,
           even_w_down, odd_norm_mix, odd_w_in, odd_g_cq, odd_g_ckv, odd_w_uq, odd_w_ukv, odd_w_o,
           odd_norm_ffn, odd_w_router, odd_w_exp_gate, odd_w_exp_up, odd_w_exp_down, final_norm):
    batch, seq, d = x.shape
    assert seq % (max(DIL_RATES) * DIL_BLOCK) == 0
    even = prep_even(dict(norm_mix=even_norm_mix, w_in=even_w_in, b_f=even_b_f, w_o=even_w_o,
                          norm_ffn=even_norm_ffn, w_gate=even_w_gate, w_up=even_w_up,
                          w_down=even_w_down))
    odd = prep_odd(dict(norm_mix=odd_norm_mix, w_in=odd_w_in, g_cq=odd_g_cq, g_ckv=odd_g_ckv,
                        w_uq=odd_w_uq, w_ukv=odd_w_ukv, w_o=odd_w_o, norm_ffn=odd_norm_ffn,
                        w_router=odd_w_router, w_exp_gate=odd_w_exp_gate, w_exp_up=odd_w_exp_up,
                        w_exp_down=odd_w_exp_down))
    depth = even_w_in.shape[0] + odd_w_in.shape[0]
    tables = rope_tables(seq)
    assert depth % 2 == 0, "the final norm is fused into the last odd layer"
    h = x.reshape(batch * seq, d)
    hn = rmsnorm(h, even_norm_mix[0], BF16)
    for layer in range(depth):
        i = layer // 2
        if layer % 2 == 0:
            h, hn = even_layer(h, hn, even, i, odd_norm_mix[i], batch=batch, seq=seq)
        elif layer + 1 < depth:
            h, hn = odd_layer(h, hn, odd, i, tables, even_norm_mix[i + 1], BF16, batch=batch, seq=seq)
        else:
            h, hn = odd_layer(h, hn, odd, i, tables, final_norm, x.dtype, batch=batch, seq=seq)
    return hn.reshape(batch, seq, d)
```

```python
import functools

import jax
import jax.numpy as jnp
from jax import lax
from jax.experimental import pallas as pl
from jax.experimental.pallas import tpu as pltpu

F32 = jnp.float32
BF16 = jnp.bfloat16

LANE = 128
SUBLANE = 8
HEAD_DIM = 128
N_HEADS = 8
Q_LORA = 512
KV_LORA = 512
ROPE_DIM = 64
MLA_QK = 256
MLA_SCALE_DIM = 192
ROPE_THETA = 10000.0
DIL_RATES = (1, 4, 16)
DIL_BLOCK = 128
N_EXPERTS = 8
NORM_EPS = 1e-6
NEG_INF = float("-inf")
LOG2E = 1.4426950408889634

ROW_TILE = 512
MM_TILE = 1024
ATT_TQ = 512
ATT_TK = 512
SB_TK = 256
SB_DEAD_LOG2 = -160.0
FFN_TM = 512
FFN_TF = 512
MOE_TM = 1024
MOE_SUB = 512
MOE_TF = 256
DMA_ROWS = 256
DMA_UNROLL = 4


def _rms(x, g):
    ms = jnp.mean(x * x, axis=-1, keepdims=True)
    return x * lax.rsqrt(ms + NORM_EPS) * g


def _log_sigmoid(x):
    return -(jnp.maximum(-x, 0.0) + jnp.log1p(jnp.exp(-jnp.abs(x))))


def _rmsnorm_body(x_ref, g_ref, o_ref):
    o_ref[...] = _rms(x_ref[...].astype(F32), g_ref[...]).astype(o_ref.dtype)


def rmsnorm(x, g, out_dtype, *, col_block=0, width=None):
    m = x.shape[0]
    width = x.shape[1] if width is None else width
    tm = min(ROW_TILE, m)
    return pl.pallas_call(
        _rmsnorm_body,
        grid=(m // tm,),
        in_specs=[pl.BlockSpec((tm, width), lambda i: (i, col_block)),
                  pl.BlockSpec((1, width), lambda i: (0, 0))],
        out_specs=pl.BlockSpec((tm, width), lambda i: (i, 0)),
        out_shape=jax.ShapeDtypeStruct((m, width), out_dtype),
        name="rmsnorm",
    )(x, g.reshape(1, width).astype(F32))


def _matmul_body(*refs, n_in, has_res):
    o_ref = refs[-1]
    acc = jnp.dot(refs[0][...], refs[n_in][...], preferred_element_type=F32)
    for i in range(1, n_in):
        acc = acc + jnp.dot(refs[i][...], refs[n_in + i][...], preferred_element_type=F32)
    if has_res:
        acc = refs[2 * n_in][...] + acc
    o_ref[...] = acc.astype(o_ref.dtype)


def matmul(a_list, w_list, out_dtype, res=None, *, layer=0, n=None):
    m = a_list[0].shape[0]
    n = w_list[0][0].shape[2] if n is None else n
    tm = min(MM_TILE, m)
    tn = MM_TILE if n % MM_TILE == 0 else n

    def w_spec(a, row_block):
        return pl.BlockSpec((None, a.shape[1], tn), lambda i, j: (layer, row_block, j))

    in_specs = [pl.BlockSpec((tm, a.shape[1]), lambda i, j: (i, 0)) for a in a_list]
    in_specs += [w_spec(a, rb) for a, (_, rb) in zip(a_list, w_list)]
    args = list(a_list) + [w for w, _ in w_list]
    if res is not None:
        in_specs.append(pl.BlockSpec((tm, tn), lambda i, j: (i, j)))
        args.append(res)
    return pl.pallas_call(
        functools.partial(_matmul_body, n_in=len(a_list), has_res=res is not None),
        grid=(m // tm, n // tn),
        in_specs=in_specs,
        out_specs=pl.BlockSpec((tm, tn), lambda i, j: (i, j)),
        out_shape=jax.ShapeDtypeStruct((m, n), out_dtype),
        name="matmul",
    )(*args)


def _out_proj_norm_body(*refs, n_in):
    res_ref, g_ref, h_ref, hn_ref = refs[2 * n_in:]
    acc = jnp.dot(refs[0][...], refs[n_in][...], preferred_element_type=F32)
    for i in range(1, n_in):
        acc = acc + jnp.dot(refs[i][...], refs[n_in + i][...], preferred_element_type=F32)
    h = res_ref[...] + acc
    h_ref[...] = h
    hn_ref[...] = _rms(h, g_ref[...]).astype(hn_ref.dtype)


def out_proj_norm(a_list, w, res, g, *, layer):
    m, d = res.shape
    tm = min(ROW_TILE, m)

    def w_spec(a, row_block):
        return pl.BlockSpec((None, a.shape[1], d), lambda i: (layer, row_block, 0))

    row = pl.BlockSpec((tm, d), lambda i: (i, 0))
    return pl.pallas_call(
        functools.partial(_out_proj_norm_body, n_in=len(a_list)),
        grid=(m // tm,),
        in_specs=[pl.BlockSpec((tm, a.shape[1]), lambda i: (i, 0)) for a in a_list]
        + [w_spec(a, rb) for rb, a in enumerate(a_list)]
        + [row, pl.BlockSpec((1, d), lambda i: (0, 0))],
        out_specs=[row, row],
        out_shape=[jax.ShapeDtypeStruct((m, d), F32), jax.ShapeDtypeStruct((m, d), BF16)],
        name="out_proj_norm",
    )(*a_list, *([w] * len(a_list)), res, g.reshape(1, d).astype(F32))


def _decay_body(fg_ref, b_ref, c_ref, *, seq):
    bias = b_ref[...]
    row = lax.broadcasted_iota(jnp.int32, (SUBLANE, LANE), 0)

    def body(i, carry):
        st = pl.multiple_of(i * SUBLANE, SUBLANE)
        x = _log_sigmoid(fg_ref[pl.ds(st, SUBLANE), :] + bias)
        for s in (1, 2, 4):
            x = x + jnp.where(row >= s, pltpu.roll(x, s, 0), 0.0)
        x = x + carry
        c_ref[pl.ds(st, SUBLANE), :] = x
        return jnp.broadcast_to(x[SUBLANE - 1:SUBLANE, :], (SUBLANE, LANE))

    lax.fori_loop(0, seq // SUBLANE, body, jnp.zeros((SUBLANE, LANE), F32), unroll=8)


def decay_cumsum(fg, b_f, batch, seq):
    bias = jnp.zeros((1, LANE), F32).at[0, :b_f.shape[0]].set(b_f.astype(F32))
    return pl.pallas_call(
        functools.partial(_decay_body, seq=seq),
        grid=(batch,),
        in_specs=[pl.BlockSpec((seq, LANE), lambda b: (b, 0)),
                  pl.BlockSpec((1, LANE), lambda b: (0, 0))],
        out_specs=pl.BlockSpec((seq, LANE), lambda b: (b, 0)),
        out_shape=jax.ShapeDtypeStruct((batch * seq, LANE), F32),
        name="decay_cumsum",
    )(fg, bias)


def _flash_body(*refs, tq, tk, scale, has_decay):
    if has_decay:
        q_ref, k_ref, v_ref, c_ref, crow_ref, o_ref = refs
    else:
        q_ref, k_ref, v_ref, o_ref = refs
    h = pl.program_id(1)
    qi = pl.program_id(2)
    q = q_ref[...]
    to_log2 = scale * LOG2E
    if has_decay:
        cq = crow_ref[:, pl.ds(pl.multiple_of(qi * tq, tq), tq)] * LOG2E

    def span(kb, n, carry, masked):
        m, l, acc = carry
        start = pl.multiple_of(kb * tk, tk)
        k = k_ref[pl.ds(start, n * tk), :]
        t_all = lax.dot_general(k, q, (((1,), (1,)), ((), ())), preferred_element_type=F32) * to_log2
        if has_decay:
            cblk = c_ref[pl.ds(start, n * tk), :]
            lane = lax.broadcasted_iota(jnp.int32, cblk.shape, 1)
            ck = jnp.sum(jnp.where(lane == h, cblk, 0.0), axis=-1, keepdims=True)
            t_all = t_all - ck * LOG2E
        for j in range(n):
            t = t_all[j * tk:(j + 1) * tk, :]
            if masked:
                key = lax.broadcasted_iota(jnp.int32, (tk, tq), 0)
                qry = lax.broadcasted_iota(jnp.int32, (tk, tq), 1)
                t = jnp.where(key <= qry, t, NEG_INF)
            top = jnp.max(t, axis=0, keepdims=True)
            if has_decay:
                top = top + cq
            m_new = jnp.maximum(m, top)
            shift = m_new - cq if has_decay else m_new
            alpha = jnp.exp2(m - m_new)
            p = jnp.exp2(t - shift)
            l = alpha * l + jnp.sum(p, axis=0, keepdims=True)
            v = v_ref[pl.ds(pl.multiple_of(start + j * tk, tk), tk), :]
            pv = lax.dot_general(v, p.astype(BF16), (((0,), (0,)), ((), ())),
                                 preferred_element_type=F32)
            acc = alpha * acc + pv
            m = m_new
        return m, l, acc

    init = (jnp.full((1, tq), NEG_INF, F32), jnp.zeros((1, tq), F32),
            jnp.zeros((v_ref.shape[1], tq), F32))
    carry = lax.fori_loop(0, qi // 4, lambda i, c: span(4 * i, 4, c, False), init)
    carry = lax.cond(qi % 4 >= 2, lambda c: span((qi // 4) * 4, 2, c, False), lambda c: c, carry)
    carry = lax.cond(qi % 2 == 1, lambda c: span(qi - 1, 1, c, False), lambda c: c, carry)
    m, l, acc = span(qi, 1, carry, True)
    o_ref[...] = (acc / l).T.astype(o_ref.dtype)


def flash_attention(q_arr, k_arr, v_arr, *, batch, seq, dqk, q_col, k_col, v_col, scale,
                    decay=None):
    tq = tk = min(ATT_TQ, seq)
    nq = seq // tq
    in_specs = [pl.BlockSpec((tq, dqk), lambda b, h, i: (b * nq + i, q_col(h))),
                pl.BlockSpec((seq, dqk), lambda b, h, i: (b, k_col(h))),
                pl.BlockSpec((seq, HEAD_DIM), lambda b, h, i: (b, v_col(h)))]
    args = [q_arr, k_arr, v_arr]
    if decay is not None:
        in_specs += [pl.BlockSpec((seq, LANE), lambda b, h, i: (b, 0)),
                     pl.BlockSpec((None, None, 1, seq), lambda b, h, i: (b, h, 0, 0))]
        args += list(decay)
    return pl.pallas_call(
        functools.partial(_flash_body, tq=tq, tk=tk, scale=scale, has_decay=decay is not None),
        grid=(batch, N_HEADS, nq),
        in_specs=in_specs,
        out_specs=pl.BlockSpec((tq, HEAD_DIM), lambda b, h, i: (b * nq + i, h)),
        out_shape=jax.ShapeDtypeStruct((batch * seq, N_HEADS * HEAD_DIM), BF16),
        name="flash_attention",
    )(*args)


def _sb_body(q_ref, k_ref, v_ref, o_ref, *, tq, tk, scale):
    qi = pl.program_id(2)
    q = q_ref[...]
    jrow = lax.broadcasted_iota(jnp.int32, (tk, tk), 0)
    scol = lax.broadcasted_iota(jnp.int32, (tk, tk), 1)
    later = jnp.where(jrow > scol, 1.0, 0.0).astype(BF16)
    later2 = jnp.concatenate([later, later], axis=0)

    def span(q_block, n_q, carry, masked):
        run, acc = carry
        width = n_q * tq
        base = pl.multiple_of(q_block * tq, tq)
        k = k_ref[pl.ds(base, width), :]
        v = v_ref[pl.ds(base, width), :]
        z = lax.dot_general(q, k, (((1,), (1,)), ((), ())), preferred_element_type=F32) * (scale * LOG2E)
        sp = jnp.maximum(z, 0.0) + jnp.log2(1.0 + jnp.exp2(-jnp.abs(z)))
        l1m = -sp
        if masked:
            strict = (lax.broadcasted_iota(jnp.int32, (tq, width), 1)
                      < lax.broadcasted_iota(jnp.int32, (tq, width), 0))
            l1m = jnp.where(strict, l1m, 0.0)
        hi = l1m.astype(BF16)
        lo = (l1m - hi.astype(F32)).astype(BF16)
        blocks = [slice(j * tk, (j + 1) * tk) for j in range(width // tk)]
        after = [jnp.dot(jnp.concatenate([hi[:, b], lo[:, b]], axis=1), later2,
                         preferred_element_type=F32) for b in blocks]
        sums = [jnp.sum(l1m[:, b], axis=-1, keepdims=True) for b in blocks]
        logw = []
        for j in reversed(range(len(blocks))):
            logw.append((z[:, blocks[j]] - sp[:, blocks[j]]) + after[j] + run)
            run = run + sums[j]
        a = jnp.exp2(jnp.concatenate(logw[::-1], axis=1))
        if masked:
            a = jnp.where(strict, a, 0.0)
        acc = acc + jnp.dot(a.astype(BF16), v, preferred_element_type=F32)
        return run, acc

    carry = (jnp.zeros((tq, 1), F32), jnp.zeros((tq, HEAD_DIM), F32))
    run, acc = span(qi, 1, carry, True)

    def live(state):
        q_block, run, _ = state
        return (q_block >= 0) & (jnp.max(run) > SB_DEAD_LOG2)

    def earlier(state):
        q_block, run, acc = state
        run, acc = span(q_block, 1, (run, acc), False)
        return q_block - 1, run, acc

    _, _, acc = lax.while_loop(live, earlier, (qi - 1, run, acc))
    o_ref[...] = acc.astype(o_ref.dtype)


def stick_breaking_attention(qkv, *, batch, seq, q_col, k_col, v_col):
    tq = min(ATT_TQ, seq)
    tk = min(SB_TK, tq)
    nq = seq // tq
    return pl.pallas_call(
        functools.partial(_sb_body, tq=tq, tk=tk, scale=HEAD_DIM ** -0.5),
        grid=(batch, N_HEADS, nq),
        in_specs=[pl.BlockSpec((tq, HEAD_DIM), lambda b, h, i: (b * nq + i, q_col(h))),
                  pl.BlockSpec((seq, HEAD_DIM), lambda b, h, i: (b, k_col(h))),
                  pl.BlockSpec((seq, HEAD_DIM), lambda b, h, i: (b, v_col(h)))],
        out_specs=pl.BlockSpec((tq, HEAD_DIM), lambda b, h, i: (b * nq + i, h)),
        out_shape=jax.ShapeDtypeStruct((batch * seq, N_HEADS * HEAD_DIM), BF16),
        name="stick_breaking",
    )(qkv, qkv, qkv)


def _mla_prep_body(q_ref, kr_ref, kv_ref, cos_ref, sin_lo_ref, sin_hi_ref, qf_ref, kf_ref):
    cos, sin_lo, sin_hi = cos_ref[...], sin_lo_ref[...], sin_hi_ref[...]

    def rope(x):
        return x * cos + pltpu.roll(x, LANE - ROPE_DIM // 2, 1) * sin_lo \
            + pltpu.roll(x, ROPE_DIM // 2, 1) * sin_hi

    k_pe = rope(kr_ref[...]).astype(BF16)
    for h in range(N_HEADS):
        lo = h * MLA_QK
        qf_ref[:, lo:lo + HEAD_DIM] = q_ref[:, lo:lo + HEAD_DIM].astype(BF16)
        qf_ref[:, lo + HEAD_DIM:lo + MLA_QK] = rope(q_ref[:, lo + HEAD_DIM:lo + MLA_QK]).astype(BF16)
        kf_ref[:, lo:lo + HEAD_DIM] = kv_ref[:, lo:lo + HEAD_DIM]
        kf_ref[:, lo + HEAD_DIM:lo + MLA_QK] = k_pe


def mla_prep(q_raw, lat, kv, tables, *, seq):
    m = q_raw.shape[0]
    tm = min(ROW_TILE, seq)
    npos = seq // tm
    wide = N_HEADS * MLA_QK
    kr_block = (Q_LORA + KV_LORA) // LANE
    tab_spec = pl.BlockSpec((tm, LANE), lambda i: (i % npos, 0))
    return pl.pallas_call(
        _mla_prep_body,
        grid=(m // tm,),
        in_specs=[pl.BlockSpec((tm, wide), lambda i: (i, 0)),
                  pl.BlockSpec((tm, LANE), lambda i: (i, kr_block)),
                  pl.BlockSpec((tm, wide), lambda i: (i, 0)),
                  tab_spec, tab_spec, tab_spec],
        out_specs=[pl.BlockSpec((tm, wide), lambda i: (i, 0)),
                   pl.BlockSpec((tm, wide), lambda i: (i, 0))],
        out_shape=[jax.ShapeDtypeStruct((m, wide), BF16),
                   jax.ShapeDtypeStruct((m, wide), BF16)],
        name="mla_prep",
    )(q_raw, lat, kv, *tables)


def rope_tables(seq):
    half = ROPE_DIM // 2
    inv_freq = ROPE_THETA ** (-jnp.arange(half, dtype=F32) / half)
    ang = jnp.arange(seq, dtype=F32)[:, None] * inv_freq[None, :]
    cos, sin = jnp.cos(ang), jnp.sin(ang)
    zeros = jnp.zeros((seq, half), F32)
    pad = jnp.zeros((seq, LANE - ROPE_DIM), F32)
    cos_t = jnp.concatenate([cos, cos, pad], axis=1)
    sin_lo = jnp.concatenate([-sin, zeros, pad], axis=1)
    sin_hi = jnp.concatenate([zeros, sin, pad], axis=1)
    return cos_t, sin_lo, sin_hi


def _dilated_body(q_ref, kp_ref, kc_ref, vp_ref, vc_ref, o_ref, lse_ref, *, rate, scale):
    ib = pl.program_id(2)
    n = DIL_BLOCK
    qi = lax.broadcasted_iota(jnp.int32, (n, 2 * n), 0)
    ki = lax.broadcasted_iota(jnp.int32, (n, 2 * n), 1)
    steps = n + qi - ki
    valid = (steps >= 0) & (steps <= n) & ((ki >= n) | (ib > 0))
    dist = (steps * rate).astype(F32)
    lane = lax.broadcasted_iota(jnp.int32, (n, LANE), 1)
    lse_all = jnp.zeros((n, LANE), F32)
    for h in range(N_HEADS):
        cols = slice(h * HEAD_DIM, (h + 1) * HEAD_DIM)
        q = q_ref[:, cols]
        k = jnp.concatenate([kp_ref[:, cols], kc_ref[:, cols]], axis=0)
        v = jnp.concatenate([vp_ref[:, cols], vc_ref[:, cols]], axis=0)
        s = lax.dot_general(q, k, (((1,), (1,)), ((), ())), preferred_element_type=F32) * scale
        s = jnp.where(valid, s - (2.0 ** -(h + 1)) * dist, NEG_INF)
        m = jnp.max(s, axis=-1, keepdims=True)
        p = jnp.exp(s - m)
        l = jnp.sum(p, axis=-1, keepdims=True)
        o = jnp.dot(p.astype(BF16), v, preferred_element_type=F32) / l
        o_ref[:, cols] = o.astype(o_ref.dtype)
        lse_all = jnp.where(lane == h, m + jnp.log(l), lse_all)
    lse_ref[...] = lse_all


def _dilated_proj_body(a_ref, w_ref, *refs):
    out_refs, acc_ref = refs[:-1], refs[-1]
    acc = jnp.dot(a_ref[...], w_ref[...], preferred_element_type=F32)
    n_slabs, tm, _ = acc_ref.shape
    for j in range(n_slabs):
        acc_ref[j] = acc[:, j * LANE:(j + 1) * LANE]
    for rate, o_ref in zip(DIL_RATES, out_refs):
        if rate == 1:
            o_ref[...] = acc.astype(o_ref.dtype)
            continue
        for c in range(rate):
            for j in range(n_slabs):
                o_ref[c, :, j * LANE:(j + 1) * LANE] = (
                    acc_ref[j, pl.ds(c, tm // rate, stride=rate), :].astype(o_ref.dtype))


def dilated_proj(a, w, *, layer):
    m, k = a.shape
    n = w.shape[2]
    tm, tn = min(MM_TILE, m), MM_TILE

    def out_spec(rate):
        if rate == 1:
            return pl.BlockSpec((tm, tn), lambda i, j: (i, j))
        return pl.BlockSpec((rate, tm // rate, tn), lambda i, j: (0, i, j))

    def out_shape(rate):
        return jax.ShapeDtypeStruct((m, n) if rate == 1 else (rate, m // rate, n), BF16)

    return pl.pallas_call(
        _dilated_proj_body,
        grid=(m // tm, n // tn),
        in_specs=[pl.BlockSpec((tm, k), lambda i, j: (i, 0)),
                  pl.BlockSpec((None, k, tn), lambda i, j: (layer, 0, j))],
        out_specs=[out_spec(r) for r in DIL_RATES],
        out_shape=[out_shape(r) for r in DIL_RATES],
        scratch_shapes=[pltpu.VMEM((tn // LANE, tm, LANE), F32)],
        name="dilated_proj",
    )(a, w)


def dilated_branch(dqkv, *, batch, seq, rate):
    wide = N_HEADS * HEAD_DIM
    n_rows = batch * seq
    nb = seq // rate // DIL_BLOCK
    rows = dqkv.reshape(n_rows, 3 * wide)

    def row_block(b, c, i):
        return (c * batch + b) * nb + i

    def spec(which, prev):
        def index(b, c, i):
            return (row_block(b, c, jnp.maximum(i - 1, 0) if prev else i), which)
        return pl.BlockSpec((DIL_BLOCK, wide), index)

    o, lse = pl.pallas_call(
        functools.partial(_dilated_body, rate=rate, scale=HEAD_DIM ** -0.5),
        grid=(batch, rate, nb),
        in_specs=[spec(0, False), spec(1, True), spec(1, False), spec(2, True), spec(2, False)],
        out_specs=[pl.BlockSpec((DIL_BLOCK, wide), lambda b, c, i: (row_block(b, c, i), 0)),
                   pl.BlockSpec((DIL_BLOCK, LANE), lambda b, c, i: (row_block(b, c, i), 0))],
        out_shape=[jax.ShapeDtypeStruct((n_rows, wide), BF16),
                   jax.ShapeDtypeStruct((n_rows, LANE), F32)],
        name="dilated_branch",
    )(rows, rows, rows, rows, rows)
    if rate == 1:
        return o, lse
    return o.reshape(rate, n_rows // rate, wide), lse.reshape(rate, n_rows // rate, LANE)


def _dilated_merge_body(*refs):
    n = len(DIL_RATES)
    o_refs, l_refs, out_ref = refs[:n], refs[n:2 * n], refs[2 * n]
    scratch = refs[2 * n + 1:]
    outs, lses = [], []
    for g, rate in enumerate(DIL_RATES):
        if rate == 1:
            outs.append(o_refs[g][...].astype(F32))
            lses.append(l_refs[g][...])
            continue
        o_nat, l_nat = scratch[2 * (g - 1)], scratch[2 * (g - 1) + 1]
        rows = l_nat.shape[0] // rate
        for c in range(rate):
            for h in range(N_HEADS):
                o_nat[h, pl.ds(c, rows, stride=rate), :] = (
                    o_refs[g][c, :, h * HEAD_DIM:(h + 1) * HEAD_DIM].astype(F32))
            l_nat[pl.ds(c, rows, stride=rate), :] = l_refs[g][c]
        outs.append(o_nat)
        lses.append(l_nat[...])
    top = functools.reduce(jnp.maximum, lses)
    es = [jnp.exp(x - top) for x in lses]
    den = functools.reduce(lambda a, b: a + b, es)
    ws = [e / den for e in es]
    for h in range(N_HEADS):
        cols = slice(h * HEAD_DIM, (h + 1) * HEAD_DIM)
        acc = ws[0][:, h:h + 1] * outs[0][:, cols]
        for g in range(1, n):
            acc = acc + ws[g][:, h:h + 1] * outs[g][h]
        out_ref[:, cols] = acc.astype(out_ref.dtype)


def dilated_merge(outs, lses):
    m = outs[0].shape[0]
    wide = N_HEADS * HEAD_DIM
    tm = min(ROW_TILE, m)

    def spec(rate, width):
        if rate == 1:
            return pl.BlockSpec((tm, width), lambda i: (i, 0))
        return pl.BlockSpec((rate, tm // rate, width), lambda i: (0, i, 0))

    scratch = []
    for rate in DIL_RATES[1:]:
        scratch += [pltpu.VMEM((N_HEADS, tm, HEAD_DIM), F32), pltpu.VMEM((tm, LANE), F32)]
    return pl.pallas_call(
        _dilated_merge_body,
        grid=(m // tm,),
        in_specs=[spec(r, wide) for r in DIL_RATES] + [spec(r, LANE) for r in DIL_RATES],
        out_specs=pl.BlockSpec((tm, wide), lambda i: (i, 0)),
        out_shape=jax.ShapeDtypeStruct((m, wide), BF16),
        scratch_shapes=scratch,
        name="dilated_merge",
    )(*outs, *lses)


def _swiglu_partial(x, wg, wu, wd):
    g = jnp.dot(x, wg, preferred_element_type=F32)
    u = jnp.dot(x, wu, preferred_element_type=F32)
    mid = (g * (1.0 / (1.0 + jnp.exp(-g))) * u).astype(BF16)
    return jnp.dot(mid, wd, preferred_element_type=F32)


def _dense_ffn_body(x_ref, wg_ref, wu_ref, wd_ref, res_ref, g_ref, o_ref, on_ref):
    f = pl.program_id(1)

    @pl.when(f == 0)
    def _():
        o_ref[...] = res_ref[...]

    o_ref[...] += _swiglu_partial(x_ref[...], wg_ref[...], wu_ref[...], wd_ref[...])

    @pl.when(f == pl.num_programs(1) - 1)
    def _():
        on_ref[...] = _rms(o_ref[...], g_ref[...]).astype(on_ref.dtype)


def dense_ffn(x, w_gate, w_up, w_down, res, g_next, *, layer):
    m, d = x.shape
    ff = w_gate.shape[2]
    tm, tf = min(FFN_TM, m), min(FFN_TF, ff)
    row = pl.BlockSpec((tm, d), lambda i, f: (i, 0))
    return pl.pallas_call(
        _dense_ffn_body,
        grid=(m // tm, ff // tf),
        in_specs=[row,
                  pl.BlockSpec((None, d, tf), lambda i, f: (layer, 0, f)),
                  pl.BlockSpec((None, d, tf), lambda i, f: (layer, 0, f)),
                  pl.BlockSpec((None, tf, d), lambda i, f: (layer, f, 0)),
                  row,
                  pl.BlockSpec((1, d), lambda i, f: (0, 0))],
        out_specs=[row, row],
        out_shape=[jax.ShapeDtypeStruct((m, d), F32), jax.ShapeDtypeStruct((m, d), BF16)],
        name="dense_ffn",
    )(x, w_gate, w_up, w_down, res, g_next.reshape(1, d).astype(F32))


def _pack_bf16_pairs(lo, hi):
    lo_bits = lax.bitcast_convert_type(lo.astype(BF16).astype(F32), jnp.uint32)
    hi_bits = lax.bitcast_convert_type(hi.astype(BF16).astype(F32), jnp.uint32)
    return (lo_bits >> 16) | (hi_bits & jnp.uint32(0xFFFF0000))


def _unpack_bf16_pairs(packed):
    lo = lax.bitcast_convert_type(packed << 16, F32).astype(BF16)
    hi = lax.bitcast_convert_type(packed & jnp.uint32(0xFFFF0000), F32).astype(BF16)
    return lo, hi


def _grouped_ffn_body(meta_ref, x_ref, wg_ref, wu_ref, wd_ref, o_ref, xb_ref, *, n_tiles, sub):
    i = pl.program_id(0)
    f = pl.program_id(1)
    in_use = i < meta_ref[n_tiles]
    rows = meta_ref[n_tiles + 1 + i]
    half = x_ref.shape[1]

    @pl.when(f == 0)
    def _():
        o_ref[...] = jnp.zeros_like(o_ref)

    @pl.when(in_use & (f == 0))
    def _():
        lo, hi = _unpack_bf16_pairs(x_ref[...])
        xb_ref[:, :half] = lo
        xb_ref[:, half:] = hi

    tm = x_ref.shape[0]
    half_sub = sub // 2

    def run_blocks(blocks):
        wg, wu, wd = (w[...].astype(BF16) for w in (wg_ref, wu_ref, wd_ref))
        for start, size in blocks:
            sl = pl.ds(start, size)
            o_ref[sl, :] += _swiglu_partial(xb_ref[sl, :], wg, wu, wd)

    for n_half in range(1, tm // half_sub + 1):
        full, tail = divmod(n_half, 2)
        blocks = [(b * sub, sub) for b in range(full)] + [(full * sub, half_sub)] * tail
        lo, hi = (n_half - 1) * half_sub, n_half * half_sub
        pl.when(in_use & (rows > lo) & (rows <= hi))(functools.partial(run_blocks, blocks))


def grouped_ffn(xs, meta, w_gate, w_up, w_down, *, layer, tm):
    ns, half = xs.shape
    d = 2 * half
    n_tiles = ns // tm
    ff = w_gate.shape[3]
    tf = min(MOE_TF, ff)
    nf = ff // tf

    def tile(i, meta):
        return jnp.maximum(jnp.minimum(i, meta[n_tiles] - 1), 0)

    def expert(i, meta):
        return meta[tile(i, meta)]

    def chunk(i, f, meta):
        return jnp.where(i < meta[n_tiles], f, nf - 1)

    grid_spec = pltpu.PrefetchScalarGridSpec(
        num_scalar_prefetch=1,
        grid=(n_tiles, nf),
        in_specs=[pl.BlockSpec((tm, half), lambda i, f, meta: (tile(i, meta), 0)),
                  pl.BlockSpec((None, None, d, tf),
                               lambda i, f, meta: (layer, expert(i, meta), 0, chunk(i, f, meta))),
                  pl.BlockSpec((None, None, d, tf),
                               lambda i, f, meta: (layer, expert(i, meta), 0, chunk(i, f, meta))),
                  pl.BlockSpec((None, None, tf, d),
                               lambda i, f, meta: (layer, expert(i, meta), chunk(i, f, meta), 0))],
        out_specs=pl.BlockSpec((tm, d), lambda i, f, meta: (i, 0)),
        scratch_shapes=[pltpu.VMEM((tm, d), BF16)],
    )
    return pl.pallas_call(
        functools.partial(_grouped_ffn_body, n_tiles=n_tiles, sub=min(MOE_SUB, tm)),
        grid_spec=grid_spec,
        out_shape=jax.ShapeDtypeStruct((ns, d), F32),
        name="grouped_ffn",
    )(meta, xs, w_gate, w_up, w_down)


ROUTE_I1, ROUTE_I2, ROUTE_G1, ROUTE_G2, ROUTE_P1, ROUTE_P2 = range(6)


def _router_body(h_ref, g_ref, wr_ref, route_ref, cnt_ref, seen_ref, *, tr):
    @pl.when(pl.program_id(0) == 0)
    def _():
        seen_ref[...] = jnp.zeros_like(seen_ref)

    y = _rms(h_ref[...], g_ref[...])
    y_hi = y.astype(BF16)
    y_lo = (y - y_hi.astype(F32)).astype(BF16)
    w = wr_ref[...]
    w_hi = w.astype(BF16)
    w_lo = (w - w_hi.astype(F32)).astype(BF16)

    def dot_nt(a, b):
        return lax.dot_general(a, b, (((1,), (1,)), ((), ())), preferred_element_type=F32)

    logits = dot_nt(w_hi, y_hi) + (dot_nt(w_hi, y_lo) + dot_nt(w_lo, y_hi))
    expert = lax.broadcasted_iota(jnp.int32, (N_EXPERTS, tr), 0)
    m1 = jnp.max(logits, axis=0, keepdims=True)
    i1 = jnp.min(jnp.where(logits == m1, expert, N_EXPERTS), axis=0, keepdims=True)
    rest = jnp.where(expert == i1, NEG_INF, logits)
    m2 = jnp.max(rest, axis=0, keepdims=True)
    i2 = jnp.min(jnp.where(rest == m2, expert, N_EXPERTS), axis=0, keepdims=True)
    e = jnp.exp(m2 - m1)
    g1 = 1.0 / (1.0 + e)
    g2 = e / (1.0 + e)
    sel = jnp.where((expert == i1) | (expert == i2), 1.0, 0.0)
    r = lax.broadcasted_iota(jnp.int32, (tr, tr), 0)
    c = lax.broadcasted_iota(jnp.int32, (tr, tr), 1)
    earlier = jnp.where(r < c, 1.0, 0.0).astype(BF16)
    seen = seen_ref[...]
    rank = jnp.dot(sel.astype(BF16), earlier, preferred_element_type=F32) + seen[:, 0:1]
    p1 = jnp.sum(jnp.where(expert == i1, rank, 0.0), axis=0, keepdims=True)
    p2 = jnp.sum(jnp.where(expert == i2, rank, 0.0), axis=0, keepdims=True)
    seen = seen + jnp.sum(sel, axis=1, keepdims=True)
    seen_ref[...] = seen
    cnt_ref[...] = seen
    route = jnp.zeros((SUBLANE, tr), F32)
    row = lax.broadcasted_iota(jnp.int32, (SUBLANE, tr), 0)
    for idx, val in ((ROUTE_I1, i1.astype(F32)), (ROUTE_I2, i2.astype(F32)), (ROUTE_G1, g1),
                     (ROUTE_G2, g2), (ROUTE_P1, p1), (ROUTE_P2, p2)):
        route = jnp.where(row == idx, val, route)
    route_ref[...] = route


def router(h, g, w_router):
    assert N_EXPERTS == SUBLANE
    m, d = h.shape
    tr = min(ROW_TILE, m)
    return pl.pallas_call(
        functools.partial(_router_body, tr=tr),
        grid=(m // tr,),
        in_specs=[pl.BlockSpec((tr, d), lambda i: (i, 0)),
                  pl.BlockSpec((1, d), lambda i: (0, 0)),
                  pl.BlockSpec((N_EXPERTS, d), lambda i: (0, 0))],
        out_specs=[pl.BlockSpec((SUBLANE, tr), lambda i: (0, i)),
                   pl.BlockSpec((N_EXPERTS, LANE), lambda i: (0, 0))],
        out_shape=[jax.ShapeDtypeStruct((SUBLANE, m), F32),
                   jax.ShapeDtypeStruct((N_EXPERTS, LANE), F32)],
        scratch_shapes=[pltpu.VMEM((N_EXPERTS, LANE), F32)],
        name="router",
    )(h, g.reshape(1, d).astype(F32), w_router.astype(F32).T)


def _row_copies(copies, step, slot, tr, wait):
    def body(r, _):
        for queue, cp in enumerate(copies(step, slot, r)):
            if wait:
                cp.wait()
            else:
                cp.start(priority=queue)
        return 0

    lax.fori_loop(0, tr, body, 0, unroll=DMA_UNROLL)


def _scatter_body(s1_ref, s2_ref, h_ref, g_ref, xs_in_ref, xs_ref, buf_ref, sems, *, tr):
    del xs_in_ref
    i = pl.program_id(0)
    slot = i % 2
    y = _rms(h_ref[...], g_ref[...])
    half = y.shape[1] // 2
    buf_ref[slot] = _pack_bf16_pairs(y[:, :half], y[:, half:])

    def copies(step, slot, r):
        src = buf_ref.at[slot, pl.ds(r, 1)]
        t = step * tr + r
        return (pltpu.make_async_copy(src, xs_ref.at[pl.ds(s1_ref[t], 1)], sems.at[slot]),
                pltpu.make_async_copy(src, xs_ref.at[pl.ds(s2_ref[t], 1)], sems.at[slot]))

    _row_copies(copies, i, slot, tr, wait=False)

    @pl.when(i > 0)
    def _():
        _row_copies(copies, i - 1, 1 - slot, tr, wait=True)

    @pl.when(i == pl.num_programs(0) - 1)
    def _():
        _row_copies(copies, i, slot, tr, wait=True)


def scatter_rows(h, g, slot1, slot2, n_slots):
    m, d = h.shape
    tr = min(DMA_ROWS, m)
    grid_spec = pltpu.PrefetchScalarGridSpec(
        num_scalar_prefetch=2,
        grid=(m // tr,),
        in_specs=[pl.BlockSpec((tr, d), lambda i, s1, s2: (i, 0)),
                  pl.BlockSpec((1, d), lambda i, s1, s2: (0, 0)),
                  pl.BlockSpec(memory_space=pl.ANY)],
        out_specs=pl.BlockSpec(memory_space=pl.ANY),
        scratch_shapes=[pltpu.VMEM((2, tr, d // 2), jnp.uint32), pltpu.SemaphoreType.DMA((2,))],
    )
    return pl.pallas_call(
        functools.partial(_scatter_body, tr=tr),
        grid_spec=grid_spec,
        out_shape=jax.ShapeDtypeStruct((n_slots, d // 2), jnp.uint32),
        input_output_aliases={4: 0},
        compiler_params=pltpu.CompilerParams(dimension_semantics=("arbitrary",)),
        name="scatter_rows",
    )(slot1, slot2, h, g.reshape(1, d).astype(F32), jnp.zeros((n_slots, d // 2), jnp.uint32))


def _combine_body(s1_ref, s2_ref, h_ref, gate_ref, g_ref, y_ref, o_ref, on_ref, buf_ref, sems, *, tr):
    i = pl.program_id(0)
    slot = i % 2

    def copies(step, slot, r):
        t = step * tr + r
        return (pltpu.make_async_copy(y_ref.at[pl.ds(s1_ref[t], 1)], buf_ref.at[slot, 0, pl.ds(r, 1)],
                                      sems.at[slot]),
                pltpu.make_async_copy(y_ref.at[pl.ds(s2_ref[t], 1)], buf_ref.at[slot, 1, pl.ds(r, 1)],
                                      sems.at[slot]))

    @pl.when(i == 0)
    def _():
        _row_copies(copies, 0, 0, tr, wait=False)

    @pl.when(i + 1 < pl.num_programs(0))
    def _():
        _row_copies(copies, i + 1, 1 - slot, tr, wait=False)

    _row_copies(copies, i, slot, tr, wait=True)
    gates = gate_ref[...]
    h = h_ref[...] + (gates[:, 0:1] * buf_ref[slot, 0] + gates[:, 1:2] * buf_ref[slot, 1])
    o_ref[...] = h
    on_ref[...] = _rms(h, g_ref[...]).astype(on_ref.dtype)


def combine_rows(h, gates, y, slot1, slot2, g_next, norm_dtype):
    m, d = h.shape
    tr = min(DMA_ROWS, m)
    row = pl.BlockSpec((tr, d), lambda i, s1, s2: (i, 0))
    grid_spec = pltpu.PrefetchScalarGridSpec(
        num_scalar_prefetch=2,
        grid=(m // tr,),
        in_specs=[row,
                  pl.BlockSpec((tr, LANE), lambda i, s1, s2: (i, 0)),
                  pl.BlockSpec((1, d), lambda i, s1, s2: (0, 0)),
                  pl.BlockSpec(memory_space=pl.ANY)],
        out_specs=[row, row],
        scratch_shapes=[pltpu.VMEM((2, 2, tr, d), F32), pltpu.SemaphoreType.DMA((2,))],
    )
    return pl.pallas_call(
        functools.partial(_combine_body, tr=tr),
        grid_spec=grid_spec,
        out_shape=[jax.ShapeDtypeStruct((m, d), F32), jax.ShapeDtypeStruct((m, d), norm_dtype)],
        compiler_params=pltpu.CompilerParams(dimension_semantics=("arbitrary",)),
        name="combine_rows",
    )(slot1, slot2, h, gates, g_next.reshape(1, d).astype(F32), y)


def moe_ffn(h, g, w_router, w_gate, w_up, w_down, g_next, norm_dtype, *, layer):
    m, _ = h.shape
    tm = min(MOE_TM, m)
    n_tiles = -(-(2 * m) // tm) + N_EXPERTS
    route, counts = router(h, g, w_router)
    col = lambda j: route[j].astype(jnp.int32)
    cnt = counts[:, 0].astype(jnp.int32)
    tiles_per = (cnt + tm - 1) // tm
    tile_end = jnp.cumsum(tiles_per)
    tile_start = tile_end - tiles_per
    group_start = tile_start * tm
    slot1 = group_start[col(ROUTE_I1)] + col(ROUTE_P1)
    slot2 = group_start[col(ROUTE_I2)] + col(ROUTE_P2)
    tile_ids = jnp.arange(n_tiles)
    tile_expert = jnp.minimum(jnp.sum(tile_ids[:, None] >= tile_end[None, :], axis=1), N_EXPERTS - 1)
    tile_rows = jnp.clip(cnt[tile_expert] - (tile_ids - tile_start[tile_expert]) * tm, 0, tm)
    meta = jnp.concatenate([tile_expert, tile_end[-1:], tile_rows]).astype(jnp.int32)
    xs = scatter_rows(h, g, slot1, slot2, n_tiles * tm)
    ys = grouped_ffn(xs, meta, w_gate, w_up, w_down, layer=layer, tm=tm)
    gates = jnp.zeros((m, LANE), F32).at[:, :2].set(route[ROUTE_G1:ROUTE_G2 + 1].T)
    return combine_rows(h, gates, ys, slot1, slot2, g_next, norm_dtype)


def prep_even(p):
    wide = N_HEADS * HEAD_DIM
    w_in = p["w_in"]
    n_layers, d, _ = w_in.shape
    w_fg = jnp.zeros((n_layers, d, LANE), BF16).at[:, :, :N_HEADS].set(w_in[:, :, 6 * wide:].astype(BF16))
    return dict(p, w_qkv=w_in[:, :, :6 * wide].astype(BF16), w_fg=w_fg, w_o=p["w_o"].astype(BF16),
                w_gate=p["w_gate"].astype(BF16), w_up=p["w_up"].astype(BF16),
                w_down=p["w_down"].astype(BF16))


def even_layer(h, hn, p, i, g_next, *, batch, seq):
    qkv = matmul([hn], [(p["w_qkv"], 0)], BF16, layer=i)
    fg = matmul([hn], [(p["w_fg"], 0)], F32, layer=i)
    c = decay_cumsum(fg, p["b_f"][i], batch, seq)
    c_row = c[:, :N_HEADS].reshape(batch, seq, N_HEADS).transpose(0, 2, 1).reshape(batch, N_HEADS, 1, seq)
    o_a = flash_attention(qkv, qkv, qkv, batch=batch, seq=seq, dqk=HEAD_DIM,
                          q_col=lambda hd: hd, k_col=lambda hd: N_HEADS + hd,
                          v_col=lambda hd: 2 * N_HEADS + hd, scale=HEAD_DIM ** -0.5,
                          decay=(c, c_row))
    o_b = stick_breaking_attention(qkv, batch=batch, seq=seq, q_col=lambda hd: 3 * N_HEADS + hd,
                                   k_col=lambda hd: 4 * N_HEADS + hd,
                                   v_col=lambda hd: 5 * N_HEADS + hd)
    h, hn = out_proj_norm([o_a, o_b], p["w_o"], h, p["norm_ffn"][i], layer=i)
    return dense_ffn(hn, p["w_gate"], p["w_up"], p["w_down"], h, g_next, layer=i)


def prep_odd(p):
    w_in = p["w_in"]
    n_layers, d, _ = w_in.shape
    lat_w = Q_LORA + KV_LORA
    w_lat = jnp.zeros((n_layers, d, lat_w + LANE), BF16).at[:, :, :lat_w + ROPE_DIM].set(
        w_in[:, :, :lat_w + ROPE_DIM].astype(BF16))
    w_uq = jnp.zeros((n_layers, Q_LORA, N_HEADS, MLA_QK), BF16).at[:, :, :, :HEAD_DIM + ROPE_DIM].set(
        p["w_uq"].reshape(n_layers, Q_LORA, N_HEADS, HEAD_DIM + ROPE_DIM).astype(BF16))
    return dict(p, w_lat=w_lat, w_dil=w_in[:, :, lat_w + ROPE_DIM:].astype(BF16),
                w_uq=w_uq.reshape(n_layers, Q_LORA, N_HEADS * MLA_QK),
                w_ukv=p["w_ukv"].astype(BF16), w_o=p["w_o"].astype(BF16))


def odd_layer(h, hn, p, i, tables, g_next, norm_dtype, *, batch, seq):
    lat = matmul([hn], [(p["w_lat"], 0)], F32, layer=i)
    dqkv = dilated_proj(hn, p["w_dil"], layer=i)
    cqn = rmsnorm(lat, p["g_cq"][i], BF16, col_block=0, width=Q_LORA)
    ckvn = rmsnorm(lat, p["g_ckv"][i], BF16, col_block=1, width=KV_LORA)
    q_raw = matmul([cqn], [(p["w_uq"], 0)], F32, layer=i)
    kv = matmul([ckvn], [(p["w_ukv"], 0)], BF16, layer=i)
    q_full, k_full = mla_prep(q_raw, lat, kv, tables, seq=seq)
    o_c = flash_attention(q_full, k_full, kv, batch=batch, seq=seq, dqk=MLA_QK,
                          q_col=lambda hd: hd, k_col=lambda hd: hd, v_col=lambda hd: 2 * hd + 1,
                          scale=MLA_SCALE_DIM ** -0.5)
    branches = [dilated_branch(x, batch=batch, seq=seq, rate=r) for x, r in zip(dqkv, DIL_RATES)]
    o_d = dilated_merge([o for o, _ in branches], [l for _, l in branches])
    h = matmul([o_c, o_d], [(p["w_o"], 0), (p["w_o"], 1)], F32, res=h, layer=i)
    return moe_ffn(h, p["norm_ffn"][i], p["w_router"][i], p["w_exp_gate"], p["w_exp_up"],
                   p["w_exp_down"], g_next, norm_dtype, layer=i)


def kernel(x, even_norm_mix, even_w_in, even_b_f, even_w_o, even_norm_ffn, even_w_gate, even_w_up,
           even_w_down, odd_norm_mix, odd_w_in, odd_g_cq, odd_g_ckv, odd_w_uq, odd_w_ukv, odd_w_o,
           odd_norm_ffn, odd_w_router, odd_w_exp_gate, odd_w_exp_up, odd_w_exp_down, final_norm):
    batch, seq, d = x.shape
    assert seq % (max(DIL_RATES) * DIL_BLOCK) == 0
    even = prep_even(dict(norm_mix=even_norm_mix, w_in=even_w_in, b_f=even_b_f, w_o=even_w_o,
                          norm_ffn=even_norm_ffn, w_gate=even_w_gate, w_up=even_w_up,
                          w_down=even_w_down))
    odd = prep_odd(dict(norm_mix=odd_norm_mix, w_in=odd_w_in, g_cq=odd_g_cq, g_ckv=odd_g_ckv,
                        w_uq=odd_w_uq, w_ukv=odd_w_ukv, w_o=odd_w_o, norm_ffn=odd_norm_ffn,
                        w_router=odd_w_router, w_exp_gate=odd_w_exp_gate, w_exp_up=odd_w_exp_up,
                        w_exp_down=odd_w_exp_down))
    depth = even_w_in.shape[0] + odd_w_in.shape[0]
    tables = rope_tables(seq)
    assert depth % 2 == 0, "the final norm is fused into the last odd layer"
    h = x.reshape(batch * seq, d)
    hn = rmsnorm(h, even_norm_mix[0], BF16)
    for layer in range(depth):
        i = layer // 2
        if layer % 2 == 0:
            h, hn = even_layer(h, hn, even, i, odd_norm_mix[i], batch=batch, seq=seq)
        elif layer + 1 < depth:
            h, hn = odd_layer(h, hn, odd, i, tables, even_norm_mix[i + 1], BF16, batch=batch, seq=seq)
        else:
            h, hn = odd_layer(h, hn, odd, i, tables, final_norm, x.dtype, batch=batch, seq=seq)
    return hn.reshape(batch, seq, d)
```

```python
import functools

import jax
import jax.numpy as jnp
from jax import lax
from jax.experimental import pallas as pl
from jax.experimental.pallas import tpu as pltpu

F32 = jnp.float32
BF16 = jnp.bfloat16

LANE = 128
SUBLANE = 8
HEAD_DIM = 128
N_HEADS = 8
Q_LORA = 512
KV_LORA = 512
ROPE_DIM = 64
MLA_QK = 256
MLA_SCALE_DIM = 192
ROPE_THETA = 10000.0
DIL_RATES = (1, 4, 16)
DIL_BLOCK = 128
N_EXPERTS = 8
NORM_EPS = 1e-6
NEG_INF = float("-inf")
LOG2E = 1.4426950408889634

ROW_TILE = 512
MM_TILE = 1024
ATT_TQ = 512
ATT_TK = 512
SB_TK = 256
SB_DEAD_LOG2 = -160.0
FFN_TM = 512
FFN_TF = 512
MOE_TM = 1280
MOE_SUB = 512
MOE_TF = 256
DMA_ROWS = 256
DMA_UNROLL = 4


def _rms(x, g):
    ms = jnp.mean(x * x, axis=-1, keepdims=True)
    return x * lax.rsqrt(ms + NORM_EPS) * g


def _log_sigmoid(x):
    return -(jnp.maximum(-x, 0.0) + jnp.log1p(jnp.exp(-jnp.abs(x))))


def _rmsnorm_body(x_ref, g_ref, o_ref):
    o_ref[...] = _rms(x_ref[...].astype(F32), g_ref[...]).astype(o_ref.dtype)


def rmsnorm(x, g, out_dtype, *, col_block=0, width=None):
    m = x.shape[0]
    width = x.shape[1] if width is None else width
    tm = min(ROW_TILE, m)
    return pl.pallas_call(
        _rmsnorm_body,
        grid=(m // tm,),
        in_specs=[pl.BlockSpec((tm, width), lambda i: (i, col_block)),
                  pl.BlockSpec((1, width), lambda i: (0, 0))],
        out_specs=pl.BlockSpec((tm, width), lambda i: (i, 0)),
        out_shape=jax.ShapeDtypeStruct((m, width), out_dtype),
        name="rmsnorm",
    )(x, g.reshape(1, width).astype(F32))


def _matmul_body(*refs, n_in, has_res):
    o_ref = refs[-1]
    acc = jnp.dot(refs[0][...], refs[n_in][...], preferred_element_type=F32)
    for i in range(1, n_in):
        acc = acc + jnp.dot(refs[i][...], refs[n_in + i][...], preferred_element_type=F32)
    if has_res:
        acc = refs[2 * n_in][...] + acc
    o_ref[...] = acc.astype(o_ref.dtype)


def matmul(a_list, w_list, out_dtype, res=None, *, layer=0, n=None):
    m = a_list[0].shape[0]
    n = w_list[0][0].shape[2] if n is None else n
    tm = min(MM_TILE, m)
    tn = MM_TILE if n % MM_TILE == 0 else n

    def w_spec(a, row_block):
        return pl.BlockSpec((None, a.shape[1], tn), lambda i, j: (layer, row_block, j))

    in_specs = [pl.BlockSpec((tm, a.shape[1]), lambda i, j: (i, 0)) for a in a_list]
    in_specs += [w_spec(a, rb) for a, (_, rb) in zip(a_list, w_list)]
    args = list(a_list) + [w for w, _ in w_list]
    if res is not None:
        in_specs.append(pl.BlockSpec((tm, tn), lambda i, j: (i, j)))
        args.append(res)
    return pl.pallas_call(
        functools.partial(_matmul_body, n_in=len(a_list), has_res=res is not None),
        grid=(m // tm, n // tn),
        in_specs=in_specs,
        out_specs=pl.BlockSpec((tm, tn), lambda i, j: (i, j)),
        out_shape=jax.ShapeDtypeStruct((m, n), out_dtype),
        name="matmul",
    )(*args)


def _out_proj_norm_body(*refs, n_in):
    res_ref, g_ref, h_ref, hn_ref = refs[2 * n_in:]
    acc = jnp.dot(refs[0][...], refs[n_in][...], preferred_element_type=F32)
    for i in range(1, n_in):
        acc = acc + jnp.dot(refs[i][...], refs[n_in + i][...], preferred_element_type=F32)
    h = res_ref[...] + acc
    h_ref[...] = h
    hn_ref[...] = _rms(h, g_ref[...]).astype(hn_ref.dtype)


def out_proj_norm(a_list, w, res, g, *, layer):
    m, d = res.shape
    tm = min(ROW_TILE, m)

    def w_spec(a, row_block):
        return pl.BlockSpec((None, a.shape[1], d), lambda i: (layer, row_block, 0))

    row = pl.BlockSpec((tm, d), lambda i: (i, 0))
    return pl.pallas_call(
        functools.partial(_out_proj_norm_body, n_in=len(a_list)),
        grid=(m // tm,),
        in_specs=[pl.BlockSpec((tm, a.shape[1]), lambda i: (i, 0)) for a in a_list]
        + [w_spec(a, rb) for rb, a in enumerate(a_list)]
        + [row, pl.BlockSpec((1, d), lambda i: (0, 0))],
        out_specs=[row, row],
        out_shape=[jax.ShapeDtypeStruct((m, d), F32), jax.ShapeDtypeStruct((m, d), BF16)],
        name="out_proj_norm",
    )(*a_list, *([w] * len(a_list)), res, g.reshape(1, d).astype(F32))


def _decay_body(fg_ref, b_ref, c_ref, *, seq):
    bias = b_ref[...]
    row = lax.broadcasted_iota(jnp.int32, (SUBLANE, LANE), 0)

    def body(i, carry):
        st = pl.multiple_of(i * SUBLANE, SUBLANE)
        x = _log_sigmoid(fg_ref[pl.ds(st, SUBLANE), :] + bias)
        for s in (1, 2, 4):
            x = x + jnp.where(row >= s, pltpu.roll(x, s, 0), 0.0)
        x = x + carry
        c_ref[pl.ds(st, SUBLANE), :] = x
        return jnp.broadcast_to(x[SUBLANE - 1:SUBLANE, :], (SUBLANE, LANE))

    lax.fori_loop(0, seq // SUBLANE, body, jnp.zeros((SUBLANE, LANE), F32), unroll=8)


def decay_cumsum(fg, b_f, batch, seq):
    bias = jnp.zeros((1, LANE), F32).at[0, :b_f.shape[0]].set(b_f.astype(F32))
    return pl.pallas_call(
        functools.partial(_decay_body, seq=seq),
        grid=(batch,),
        in_specs=[pl.BlockSpec((seq, LANE), lambda b: (b, 0)),
                  pl.BlockSpec((1, LANE), lambda b: (0, 0))],
        out_specs=pl.BlockSpec((seq, LANE), lambda b: (b, 0)),
        out_shape=jax.ShapeDtypeStruct((batch * seq, LANE), F32),
        name="decay_cumsum",
    )(fg, bias)


def _flash_body(*refs, tq, tk, scale, has_decay):
    if has_decay:
        q_ref, k_ref, v_ref, c_ref, crow_ref, o_ref = refs
    else:
        q_ref, k_ref, v_ref, o_ref = refs
    h = pl.program_id(1)
    qi = pl.program_id(2)
    q = q_ref[...]
    to_log2 = scale * LOG2E
    if has_decay:
        cq = crow_ref[:, pl.ds(pl.multiple_of(qi * tq, tq), tq)] * LOG2E

    def span(kb, n, carry, masked):
        m, l, acc = carry
        start = pl.multiple_of(kb * tk, tk)
        k = k_ref[pl.ds(start, n * tk), :]
        t_all = lax.dot_general(k, q, (((1,), (1,)), ((), ())), preferred_element_type=F32) * to_log2
        if has_decay:
            cblk = c_ref[pl.ds(start, n * tk), :]
            lane = lax.broadcasted_iota(jnp.int32, cblk.shape, 1)
            ck = jnp.sum(jnp.where(lane == h, cblk, 0.0), axis=-1, keepdims=True)
            t_all = t_all - ck * LOG2E
        for j in range(n):
            t = t_all[j * tk:(j + 1) * tk, :]
            if masked:
                key = lax.broadcasted_iota(jnp.int32, (tk, tq), 0)
                qry = lax.broadcasted_iota(jnp.int32, (tk, tq), 1)
                t = jnp.where(key <= qry, t, NEG_INF)
            top = jnp.max(t, axis=0, keepdims=True)
            if has_decay:
                top = top + cq
            m_new = jnp.maximum(m, top)
            shift = m_new - cq if has_decay else m_new
            alpha = jnp.exp2(m - m_new)
            p = jnp.exp2(t - shift)
            l = alpha * l + jnp.sum(p, axis=0, keepdims=True)
            v = v_ref[pl.ds(pl.multiple_of(start + j * tk, tk), tk), :]
            pv = lax.dot_general(v, p.astype(BF16), (((0,), (0,)), ((), ())),
                                 preferred_element_type=F32)
            acc = alpha * acc + pv
            m = m_new
        return m, l, acc

    init = (jnp.full((1, tq), NEG_INF, F32), jnp.zeros((1, tq), F32),
            jnp.zeros((v_ref.shape[1], tq), F32))
    carry = lax.fori_loop(0, qi // 4, lambda i, c: span(4 * i, 4, c, False), init)
    carry = lax.cond(qi % 4 >= 2, lambda c: span((qi // 4) * 4, 2, c, False), lambda c: c, carry)
    carry = lax.cond(qi % 2 == 1, lambda c: span(qi - 1, 1, c, False), lambda c: c, carry)
    m, l, acc = span(qi, 1, carry, True)
    o_ref[...] = (acc / l).T.astype(o_ref.dtype)


def flash_attention(q_arr, k_arr, v_arr, *, batch, seq, dqk, q_col, k_col, v_col, scale,
                    decay=None):
    tq = tk = min(ATT_TQ, seq)
    nq = seq // tq
    in_specs = [pl.BlockSpec((tq, dqk), lambda b, h, i: (b * nq + i, q_col(h))),
                pl.BlockSpec((seq, dqk), lambda b, h, i: (b, k_col(h))),
                pl.BlockSpec((seq, HEAD_DIM), lambda b, h, i: (b, v_col(h)))]
    args = [q_arr, k_arr, v_arr]
    if decay is not None:
        in_specs += [pl.BlockSpec((seq, LANE), lambda b, h, i: (b, 0)),
                     pl.BlockSpec((None, None, 1, seq), lambda b, h, i: (b, h, 0, 0))]
        args += list(decay)
    return pl.pallas_call(
        functools.partial(_flash_body, tq=tq, tk=tk, scale=scale, has_decay=decay is not None),
        grid=(batch, N_HEADS, nq),
        in_specs=in_specs,
        out_specs=pl.BlockSpec((tq, HEAD_DIM), lambda b, h, i: (b * nq + i, h)),
        out_shape=jax.ShapeDtypeStruct((batch * seq, N_HEADS * HEAD_DIM), BF16),
        name="flash_attention",
    )(*args)


def _sb_body(q_ref, k_ref, v_ref, o_ref, *, tq, tk, scale):
    qi = pl.program_id(2)
    q = q_ref[...]
    jrow = lax.broadcasted_iota(jnp.int32, (tk, tk), 0)
    scol = lax.broadcasted_iota(jnp.int32, (tk, tk), 1)
    later = jnp.where(jrow > scol, 1.0, 0.0).astype(BF16)
    later2 = jnp.concatenate([later, later], axis=0)

    def span(q_block, n_q, carry, masked):
        run, acc = carry
        width = n_q * tq
        base = pl.multiple_of(q_block * tq, tq)
        k = k_ref[pl.ds(base, width), :]
        v = v_ref[pl.ds(base, width), :]
        z = lax.dot_general(q, k, (((1,), (1,)), ((), ())), preferred_element_type=F32) * (scale * LOG2E)
        sp = jnp.maximum(z, 0.0) + jnp.log2(1.0 + jnp.exp2(-jnp.abs(z)))
        l1m = -sp
        if masked:
            strict = (lax.broadcasted_iota(jnp.int32, (tq, width), 1)
                      < lax.broadcasted_iota(jnp.int32, (tq, width), 0))
            l1m = jnp.where(strict, l1m, 0.0)
        hi = l1m.astype(BF16)
        lo = (l1m - hi.astype(F32)).astype(BF16)
        blocks = [slice(j * tk, (j + 1) * tk) for j in range(width // tk)]
        after = [jnp.dot(jnp.concatenate([hi[:, b], lo[:, b]], axis=1), later2,
                         preferred_element_type=F32) for b in blocks]
        sums = [jnp.sum(l1m[:, b], axis=-1, keepdims=True) for b in blocks]
        logw = []
        for j in reversed(range(len(blocks))):
            logw.append((z[:, blocks[j]] - sp[:, blocks[j]]) + after[j] + run)
            run = run + sums[j]
        a = jnp.exp2(jnp.concatenate(logw[::-1], axis=1))
        if masked:
            a = jnp.where(strict, a, 0.0)
        acc = acc + jnp.dot(a.astype(BF16), v, preferred_element_type=F32)
        return run, acc

    carry = (jnp.zeros((tq, 1), F32), jnp.zeros((tq, HEAD_DIM), F32))
    run, acc = span(qi, 1, carry, True)

    def live(state):
        q_block, run, _ = state
        return (q_block >= 0) & (jnp.max(run) > SB_DEAD_LOG2)

    def earlier(state):
        q_block, run, acc = state
        run, acc = span(q_block, 1, (run, acc), False)
        return q_block - 1, run, acc

    _, _, acc = lax.while_loop(live, earlier, (qi - 1, run, acc))
    o_ref[...] = acc.astype(o_ref.dtype)


def stick_breaking_attention(qkv, *, batch, seq, q_col, k_col, v_col):
    tq = min(ATT_TQ, seq)
    tk = min(SB_TK, tq)
    nq = seq // tq
    return pl.pallas_call(
        functools.partial(_sb_body, tq=tq, tk=tk, scale=HEAD_DIM ** -0.5),
        grid=(batch, N_HEADS, nq),
        in_specs=[pl.BlockSpec((tq, HEAD_DIM), lambda b, h, i: (b * nq + i, q_col(h))),
                  pl.BlockSpec((seq, HEAD_DIM), lambda b, h, i: (b, k_col(h))),
                  pl.BlockSpec((seq, HEAD_DIM), lambda b, h, i: (b, v_col(h)))],
        out_specs=pl.BlockSpec((tq, HEAD_DIM), lambda b, h, i: (b * nq + i, h)),
        out_shape=jax.ShapeDtypeStruct((batch * seq, N_HEADS * HEAD_DIM), BF16),
        name="stick_breaking",
    )(qkv, qkv, qkv)


def _mla_prep_body(q_ref, kr_ref, kv_ref, cos_ref, sin_lo_ref, sin_hi_ref, qf_ref, kf_ref):
    cos, sin_lo, sin_hi = cos_ref[...], sin_lo_ref[...], sin_hi_ref[...]

    def rope(x):
        return x * cos + pltpu.roll(x, LANE - ROPE_DIM // 2, 1) * sin_lo \
            + pltpu.roll(x, ROPE_DIM // 2, 1) * sin_hi

    k_pe = rope(kr_ref[...]).astype(BF16)
    for h in range(N_HEADS):
        lo = h * MLA_QK
        qf_ref[:, lo:lo + HEAD_DIM] = q_ref[:, lo:lo + HEAD_DIM].astype(BF16)
        qf_ref[:, lo + HEAD_DIM:lo + MLA_QK] = rope(q_ref[:, lo + HEAD_DIM:lo + MLA_QK]).astype(BF16)
        kf_ref[:, lo:lo + HEAD_DIM] = kv_ref[:, lo:lo + HEAD_DIM]
        kf_ref[:, lo + HEAD_DIM:lo + MLA_QK] = k_pe


def mla_prep(q_raw, lat, kv, tables, *, seq):
    m = q_raw.shape[0]
    tm = min(ROW_TILE, seq)
    npos = seq // tm
    wide = N_HEADS * MLA_QK
    kr_block = (Q_LORA + KV_LORA) // LANE
    tab_spec = pl.BlockSpec((tm, LANE), lambda i: (i % npos, 0))
    return pl.pallas_call(
        _mla_prep_body,
        grid=(m // tm,),
        in_specs=[pl.BlockSpec((tm, wide), lambda i: (i, 0)),
                  pl.BlockSpec((tm, LANE), lambda i: (i, kr_block)),
                  pl.BlockSpec((tm, wide), lambda i: (i, 0)),
                  tab_spec, tab_spec, tab_spec],
        out_specs=[pl.BlockSpec((tm, wide), lambda i: (i, 0)),
                   pl.BlockSpec((tm, wide), lambda i: (i, 0))],
        out_shape=[jax.ShapeDtypeStruct((m, wide), BF16),
                   jax.ShapeDtypeStruct((m, wide), BF16)],
        name="mla_prep",
    )(q_raw, lat, kv, *tables)


def rope_tables(seq):
    half = ROPE_DIM // 2
    inv_freq = ROPE_THETA ** (-jnp.arange(half, dtype=F32) / half)
    ang = jnp.arange(seq, dtype=F32)[:, None] * inv_freq[None, :]
    cos, sin = jnp.cos(ang), jnp.sin(ang)
    zeros = jnp.zeros((seq, half), F32)
    pad = jnp.zeros((seq, LANE - ROPE_DIM), F32)
    cos_t = jnp.concatenate([cos, cos, pad], axis=1)
    sin_lo = jnp.concatenate([-sin, zeros, pad], axis=1)
    sin_hi = jnp.concatenate([zeros, sin, pad], axis=1)
    return cos_t, sin_lo, sin_hi


def _dilated_body(q_ref, kp_ref, kc_ref, vp_ref, vc_ref, o_ref, lse_ref, *, rate, scale):
    ib = pl.program_id(2)
    n = DIL_BLOCK
    qi = lax.broadcasted_iota(jnp.int32, (n, 2 * n), 0)
    ki = lax.broadcasted_iota(jnp.int32, (n, 2 * n), 1)
    steps = n + qi - ki
    valid = (steps >= 0) & (steps <= n) & ((ki >= n) | (ib > 0))
    dist = (steps * rate).astype(F32)
    lane = lax.broadcasted_iota(jnp.int32, (n, LANE), 1)
    lse_all = jnp.zeros((n, LANE), F32)
    for h in range(N_HEADS):
        cols = slice(h * HEAD_DIM, (h + 1) * HEAD_DIM)
        q = q_ref[:, cols]
        k = jnp.concatenate([kp_ref[:, cols], kc_ref[:, cols]], axis=0)
        v = jnp.concatenate([vp_ref[:, cols], vc_ref[:, cols]], axis=0)
        s = lax.dot_general(q, k, (((1,), (1,)), ((), ())), preferred_element_type=F32) * scale
        s = jnp.where(valid, s - (2.0 ** -(h + 1)) * dist, NEG_INF)
        m = jnp.max(s, axis=-1, keepdims=True)
        p = jnp.exp(s - m)
        l = jnp.sum(p, axis=-1, keepdims=True)
        o = jnp.dot(p.astype(BF16), v, preferred_element_type=F32) / l
        o_ref[:, cols] = o.astype(o_ref.dtype)
        lse_all = jnp.where(lane == h, m + jnp.log(l), lse_all)
    lse_ref[...] = lse_all


def _dilated_proj_body(a_ref, w_ref, *refs):
    out_refs, acc_ref = refs[:-1], refs[-1]
    acc = jnp.dot(a_ref[...], w_ref[...], preferred_element_type=F32)
    n_slabs, tm, _ = acc_ref.shape
    for j in range(n_slabs):
        acc_ref[j] = acc[:, j * LANE:(j + 1) * LANE]
    for rate, o_ref in zip(DIL_RATES, out_refs):
        if rate == 1:
            o_ref[...] = acc.astype(o_ref.dtype)
            continue
        for c in range(rate):
            for j in range(n_slabs):
                o_ref[c, :, j * LANE:(j + 1) * LANE] = (
                    acc_ref[j, pl.ds(c, tm // rate, stride=rate), :].astype(o_ref.dtype))


def dilated_proj(a, w, *, layer):
    m, k = a.shape
    n = w.shape[2]
    tm, tn = min(MM_TILE, m), MM_TILE

    def out_spec(rate):
        if rate == 1:
            return pl.BlockSpec((tm, tn), lambda i, j: (i, j))
        return pl.BlockSpec((rate, tm // rate, tn), lambda i, j: (0, i, j))

    def out_shape(rate):
        return jax.ShapeDtypeStruct((m, n) if rate == 1 else (rate, m // rate, n), BF16)

    return pl.pallas_call(
        _dilated_proj_body,
        grid=(m // tm, n // tn),
        in_specs=[pl.BlockSpec((tm, k), lambda i, j: (i, 0)),
                  pl.BlockSpec((None, k, tn), lambda i, j: (layer, 0, j))],
        out_specs=[out_spec(r) for r in DIL_RATES],
        out_shape=[out_shape(r) for r in DIL_RATES],
        scratch_shapes=[pltpu.VMEM((tn // LANE, tm, LANE), F32)],
        name="dilated_proj",
    )(a, w)


def dilated_branch(dqkv, *, batch, seq, rate):
    wide = N_HEADS * HEAD_DIM
    n_rows = batch * seq
    nb = seq // rate // DIL_BLOCK
    rows = dqkv.reshape(n_rows, 3 * wide)

    def row_block(b, c, i):
        return (c * batch + b) * nb + i

    def spec(which, prev):
        def index(b, c, i):
            return (row_block(b, c, jnp.maximum(i - 1, 0) if prev else i), which)
        return pl.BlockSpec((DIL_BLOCK, wide), index)

    o, lse = pl.pallas_call(
        functools.partial(_dilated_body, rate=rate, scale=HEAD_DIM ** -0.5),
        grid=(batch, rate, nb),
        in_specs=[spec(0, False), spec(1, True), spec(1, False), spec(2, True), spec(2, False)],
        out_specs=[pl.BlockSpec((DIL_BLOCK, wide), lambda b, c, i: (row_block(b, c, i), 0)),
                   pl.BlockSpec((DIL_BLOCK, LANE), lambda b, c, i: (row_block(b, c, i), 0))],
        out_shape=[jax.ShapeDtypeStruct((n_rows, wide), BF16),
                   jax.ShapeDtypeStruct((n_rows, LANE), F32)],
        name="dilated_branch",
    )(rows, rows, rows, rows, rows)
    if rate == 1:
        return o, lse
    return o.reshape(rate, n_rows // rate, wide), lse.reshape(rate, n_rows // rate, LANE)


def _dilated_merge_body(*refs):
    n = len(DIL_RATES)
    o_refs, l_refs, out_ref = refs[:n], refs[n:2 * n], refs[2 * n]
    scratch = refs[2 * n + 1:]
    outs, lses = [], []
    for g, rate in enumerate(DIL_RATES):
        if rate == 1:
            outs.append(o_refs[g][...].astype(F32))
            lses.append(l_refs[g][...])
            continue
        o_nat, l_nat = scratch[2 * (g - 1)], scratch[2 * (g - 1) + 1]
        rows = l_nat.shape[0] // rate
        for c in range(rate):
            for h in range(N_HEADS):
                o_nat[h, pl.ds(c, rows, stride=rate), :] = (
                    o_refs[g][c, :, h * HEAD_DIM:(h + 1) * HEAD_DIM].astype(F32))
            l_nat[pl.ds(c, rows, stride=rate), :] = l_refs[g][c]
        outs.append(o_nat)
        lses.append(l_nat[...])
    top = functools.reduce(jnp.maximum, lses)
    es = [jnp.exp(x - top) for x in lses]
    den = functools.reduce(lambda a, b: a + b, es)
    ws = [e / den for e in es]
    for h in range(N_HEADS):
        cols = slice(h * HEAD_DIM, (h + 1) * HEAD_DIM)
        acc = ws[0][:, h:h + 1] * outs[0][:, cols]
        for g in range(1, n):
            acc = acc + ws[g][:, h:h + 1] * outs[g][h]
        out_ref[:, cols] = acc.astype(out_ref.dtype)


def dilated_merge(outs, lses):
    m = outs[0].shape[0]
    wide = N_HEADS * HEAD_DIM
    tm = min(ROW_TILE, m)

    def spec(rate, width):
        if rate == 1:
            return pl.BlockSpec((tm, width), lambda i: (i, 0))
        return pl.BlockSpec((rate, tm // rate, width), lambda i: (0, i, 0))

    scratch = []
    for rate in DIL_RATES[1:]:
        scratch += [pltpu.VMEM((N_HEADS, tm, HEAD_DIM), F32), pltpu.VMEM((tm, LANE), F32)]
    return pl.pallas_call(
        _dilated_merge_body,
        grid=(m // tm,),
        in_specs=[spec(r, wide) for r in DIL_RATES] + [spec(r, LANE) for r in DIL_RATES],
        out_specs=pl.BlockSpec((tm, wide), lambda i: (i, 0)),
        out_shape=jax.ShapeDtypeStruct((m, wide), BF16),
        scratch_shapes=scratch,
        name="dilated_merge",
    )(*outs, *lses)


def _swiglu_partial(x, wg, wu, wd):
    g = jnp.dot(x, wg, preferred_element_type=F32)
    u = jnp.dot(x, wu, preferred_element_type=F32)
    mid = (g * (1.0 / (1.0 + jnp.exp(-g))) * u).astype(BF16)
    return jnp.dot(mid, wd, preferred_element_type=F32)


def _dense_ffn_body(x_ref, wg_ref, wu_ref, wd_ref, res_ref, g_ref, o_ref, on_ref):
    f = pl.program_id(1)

    @pl.when(f == 0)
    def _():
        o_ref[...] = res_ref[...]

    o_ref[...] += _swiglu_partial(x_ref[...], wg_ref[...], wu_ref[...], wd_ref[...])

    @pl.when(f == pl.num_programs(1) - 1)
    def _():
        on_ref[...] = _rms(o_ref[...], g_ref[...]).astype(on_ref.dtype)


def dense_ffn(x, w_gate, w_up, w_down, res, g_next, *, layer):
    m, d = x.shape
    ff = w_gate.shape[2]
    tm, tf = min(FFN_TM, m), min(FFN_TF, ff)
    row = pl.BlockSpec((tm, d), lambda i, f: (i, 0))
    return pl.pallas_call(
        _dense_ffn_body,
        grid=(m // tm, ff // tf),
        in_specs=[row,
                  pl.BlockSpec((None, d, tf), lambda i, f: (layer, 0, f)),
                  pl.BlockSpec((None, d, tf), lambda i, f: (layer, 0, f)),
                  pl.BlockSpec((None, tf, d), lambda i, f: (layer, f, 0)),
                  row,
                  pl.BlockSpec((1, d), lambda i, f: (0, 0))],
        out_specs=[row, row],
        out_shape=[jax.ShapeDtypeStruct((m, d), F32), jax.ShapeDtypeStruct((m, d), BF16)],
        name="dense_ffn",
    )(x, w_gate, w_up, w_down, res, g_next.reshape(1, d).astype(F32))


def _pack_bf16_pairs(lo, hi):
    lo_bits = lax.bitcast_convert_type(lo.astype(BF16).astype(F32), jnp.uint32)
    hi_bits = lax.bitcast_convert_type(hi.astype(BF16).astype(F32), jnp.uint32)
    return (lo_bits >> 16) | (hi_bits & jnp.uint32(0xFFFF0000))


def _unpack_bf16_pairs(packed):
    lo = lax.bitcast_convert_type(packed << 16, F32).astype(BF16)
    hi = lax.bitcast_convert_type(packed & jnp.uint32(0xFFFF0000), F32).astype(BF16)
    return lo, hi


def _grouped_ffn_body(meta_ref, x_ref, wg_ref, wu_ref, wd_ref, o_ref, xb_ref, *, n_tiles, sub):
    i = pl.program_id(0)
    f = pl.program_id(1)
    in_use = i < meta_ref[n_tiles]
    rows = meta_ref[n_tiles + 1 + i]
    half = x_ref.shape[1]

    @pl.when(f == 0)
    def _():
        o_ref[...] = jnp.zeros_like(o_ref)

    @pl.when(in_use & (f == 0))
    def _():
        lo, hi = _unpack_bf16_pairs(x_ref[...])
        xb_ref[:, :half] = lo
        xb_ref[:, half:] = hi

    tm = x_ref.shape[0]
    half_sub = sub // 2

    def run_blocks(blocks):
        wg, wu, wd = (w[...].astype(BF16) for w in (wg_ref, wu_ref, wd_ref))
        for start, size in blocks:
            sl = pl.ds(start, size)
            o_ref[sl, :] += _swiglu_partial(xb_ref[sl, :], wg, wu, wd)

    for n_half in range(1, tm // half_sub + 1):
        full, tail = divmod(n_half, 2)
        blocks = [(b * sub, sub) for b in range(full)] + [(full * sub, half_sub)] * tail
        lo, hi = (n_half - 1) * half_sub, n_half * half_sub
        pl.when(in_use & (rows > lo) & (rows <= hi))(functools.partial(run_blocks, blocks))


def grouped_ffn(xs, meta, w_gate, w_up, w_down, *, layer, tm):
    ns, half = xs.shape
    d = 2 * half
    n_tiles = ns // tm
    ff = w_gate.shape[3]
    tf = min(MOE_TF, ff)
    nf = ff // tf

    def tile(i, meta):
        return jnp.maximum(jnp.minimum(i, meta[n_tiles] - 1), 0)

    def expert(i, meta):
        return meta[tile(i, meta)]

    def chunk(i, f, meta):
        return jnp.where(i < meta[n_tiles], f, nf - 1)

    grid_spec = pltpu.PrefetchScalarGridSpec(
        num_scalar_prefetch=1,
        grid=(n_tiles, nf),
        in_specs=[pl.BlockSpec((tm, half), lambda i, f, meta: (tile(i, meta), 0),
                               pipeline_mode=pl.Buffered(1)),
                  pl.BlockSpec((None, None, d, tf),
                               lambda i, f, meta: (layer, expert(i, meta), 0, chunk(i, f, meta))),
                  pl.BlockSpec((None, None, d, tf),
                               lambda i, f, meta: (layer, expert(i, meta), 0, chunk(i, f, meta))),
                  pl.BlockSpec((None, None, tf, d),
                               lambda i, f, meta: (layer, expert(i, meta), chunk(i, f, meta), 0))],
        out_specs=pl.BlockSpec((tm, d), lambda i, f, meta: (i, 0)),
        scratch_shapes=[pltpu.VMEM((tm, d), BF16)],
    )
    return pl.pallas_call(
        functools.partial(_grouped_ffn_body, n_tiles=n_tiles, sub=min(MOE_SUB, tm)),
        grid_spec=grid_spec,
        out_shape=jax.ShapeDtypeStruct((ns, d), F32),
        name="grouped_ffn",
    )(meta, xs, w_gate, w_up, w_down)


ROUTE_I1, ROUTE_I2, ROUTE_G1, ROUTE_G2, ROUTE_P1, ROUTE_P2 = range(6)


def _router_body(h_ref, g_ref, wr_ref, route_ref, cnt_ref, seen_ref, *, tr):
    @pl.when(pl.program_id(0) == 0)
    def _():
        seen_ref[...] = jnp.zeros_like(seen_ref)

    y = _rms(h_ref[...], g_ref[...])
    y_hi = y.astype(BF16)
    y_lo = (y - y_hi.astype(F32)).astype(BF16)
    w = wr_ref[...]
    w_hi = w.astype(BF16)
    w_lo = (w - w_hi.astype(F32)).astype(BF16)

    def dot_nt(a, b):
        return lax.dot_general(a, b, (((1,), (1,)), ((), ())), preferred_element_type=F32)

    logits = dot_nt(w_hi, y_hi) + (dot_nt(w_hi, y_lo) + dot_nt(w_lo, y_hi))
    expert = lax.broadcasted_iota(jnp.int32, (N_EXPERTS, tr), 0)
    m1 = jnp.max(logits, axis=0, keepdims=True)
    i1 = jnp.min(jnp.where(logits == m1, expert, N_EXPERTS), axis=0, keepdims=True)
    rest = jnp.where(expert == i1, NEG_INF, logits)
    m2 = jnp.max(rest, axis=0, keepdims=True)
    i2 = jnp.min(jnp.where(rest == m2, expert, N_EXPERTS), axis=0, keepdims=True)
    e = jnp.exp(m2 - m1)
    g1 = 1.0 / (1.0 + e)
    g2 = e / (1.0 + e)
    sel = jnp.where((expert == i1) | (expert == i2), 1.0, 0.0)
    r = lax.broadcasted_iota(jnp.int32, (tr, tr), 0)
    c = lax.broadcasted_iota(jnp.int32, (tr, tr), 1)
    earlier = jnp.where(r < c, 1.0, 0.0).astype(BF16)
    seen = seen_ref[...]
    rank = jnp.dot(sel.astype(BF16), earlier, preferred_element_type=F32) + seen[:, 0:1]
    p1 = jnp.sum(jnp.where(expert == i1, rank, 0.0), axis=0, keepdims=True)
    p2 = jnp.sum(jnp.where(expert == i2, rank, 0.0), axis=0, keepdims=True)
    seen = seen + jnp.sum(sel, axis=1, keepdims=True)
    seen_ref[...] = seen
    cnt_ref[...] = seen
    route = jnp.zeros((SUBLANE, tr), F32)
    row = lax.broadcasted_iota(jnp.int32, (SUBLANE, tr), 0)
    for idx, val in ((ROUTE_I1, i1.astype(F32)), (ROUTE_I2, i2.astype(F32)), (ROUTE_G1, g1),
                     (ROUTE_G2, g2), (ROUTE_P1, p1), (ROUTE_P2, p2)):
        route = jnp.where(row == idx, val, route)
    route_ref[...] = route


def router(h, g, w_router):
    assert N_EXPERTS == SUBLANE
    m, d = h.shape
    tr = min(ROW_TILE, m)
    return pl.pallas_call(
        functools.partial(_router_body, tr=tr),
        grid=(m // tr,),
        in_specs=[pl.BlockSpec((tr, d), lambda i: (i, 0)),
                  pl.BlockSpec((1, d), lambda i: (0, 0)),
                  pl.BlockSpec((N_EXPERTS, d), lambda i: (0, 0))],
        out_specs=[pl.BlockSpec((SUBLANE, tr), lambda i: (0, i)),
                   pl.BlockSpec((N_EXPERTS, LANE), lambda i: (0, 0))],
        out_shape=[jax.ShapeDtypeStruct((SUBLANE, m), F32),
                   jax.ShapeDtypeStruct((N_EXPERTS, LANE), F32)],
        scratch_shapes=[pltpu.VMEM((N_EXPERTS, LANE), F32)],
        name="router",
    )(h, g.reshape(1, d).astype(F32), w_router.astype(F32).T)


def _row_copies(copies, step, slot, tr, wait):
    def body(r, _):
        for queue, cp in enumerate(copies(step, slot, r)):
            if wait:
                cp.wait()
            else:
                cp.start(priority=queue)
        return 0

    lax.fori_loop(0, tr, body, 0, unroll=DMA_UNROLL)


def _scatter_body(s1_ref, s2_ref, h_ref, g_ref, xs_in_ref, xs_ref, buf_ref, sems, *, tr):
    del xs_in_ref
    i = pl.program_id(0)
    slot = i % 2
    y = _rms(h_ref[...], g_ref[...])
    half = y.shape[1] // 2
    buf_ref[slot] = _pack_bf16_pairs(y[:, :half], y[:, half:])

    def copies(step, slot, r):
        src = buf_ref.at[slot, pl.ds(r, 1)]
        t = step * tr + r
        return (pltpu.make_async_copy(src, xs_ref.at[pl.ds(s1_ref[t], 1)], sems.at[slot]),
                pltpu.make_async_copy(src, xs_ref.at[pl.ds(s2_ref[t], 1)], sems.at[slot]))

    _row_copies(copies, i, slot, tr, wait=False)

    @pl.when(i > 0)
    def _():
        _row_copies(copies, i - 1, 1 - slot, tr, wait=True)

    @pl.when(i == pl.num_programs(0) - 1)
    def _():
        _row_copies(copies, i, slot, tr, wait=True)


def scatter_rows(h, g, slot1, slot2, n_slots):
    m, d = h.shape
    tr = min(DMA_ROWS, m)
    grid_spec = pltpu.PrefetchScalarGridSpec(
        num_scalar_prefetch=2,
        grid=(m // tr,),
        in_specs=[pl.BlockSpec((tr, d), lambda i, s1, s2: (i, 0)),
                  pl.BlockSpec((1, d), lambda i, s1, s2: (0, 0)),
                  pl.BlockSpec(memory_space=pl.ANY)],
        out_specs=pl.BlockSpec(memory_space=pl.ANY),
        scratch_shapes=[pltpu.VMEM((2, tr, d // 2), jnp.uint32), pltpu.SemaphoreType.DMA((2,))],
    )
    return pl.pallas_call(
        functools.partial(_scatter_body, tr=tr),
        grid_spec=grid_spec,
        out_shape=jax.ShapeDtypeStruct((n_slots, d // 2), jnp.uint32),
        input_output_aliases={4: 0},
        compiler_params=pltpu.CompilerParams(dimension_semantics=("arbitrary",)),
        name="scatter_rows",
    )(slot1, slot2, h, g.reshape(1, d).astype(F32), jnp.zeros((n_slots, d // 2), jnp.uint32))


def _combine_body(s1_ref, s2_ref, h_ref, gate_ref, g_ref, y_ref, o_ref, on_ref, buf_ref, sems, *, tr):
    i = pl.program_id(0)
    slot = i % 2

    def copies(step, slot, r):
        t = step * tr + r
        return (pltpu.make_async_copy(y_ref.at[pl.ds(s1_ref[t], 1)], buf_ref.at[slot, 0, pl.ds(r, 1)],
                                      sems.at[slot]),
                pltpu.make_async_copy(y_ref.at[pl.ds(s2_ref[t], 1)], buf_ref.at[slot, 1, pl.ds(r, 1)],
                                      sems.at[slot]))

    @pl.when(i == 0)
    def _():
        _row_copies(copies, 0, 0, tr, wait=False)

    @pl.when(i + 1 < pl.num_programs(0))
    def _():
        _row_copies(copies, i + 1, 1 - slot, tr, wait=False)

    _row_copies(copies, i, slot, tr, wait=True)
    gates = gate_ref[...]
    h = h_ref[...] + (gates[:, 0:1] * buf_ref[slot, 0] + gates[:, 1:2] * buf_ref[slot, 1])
    o_ref[...] = h
    on_ref[...] = _rms(h, g_ref[...]).astype(on_ref.dtype)


def combine_rows(h, gates, y, slot1, slot2, g_next, norm_dtype):
    m, d = h.shape
    tr = min(DMA_ROWS, m)
    row = pl.BlockSpec((tr, d), lambda i, s1, s2: (i, 0))
    grid_spec = pltpu.PrefetchScalarGridSpec(
        num_scalar_prefetch=2,
        grid=(m // tr,),
        in_specs=[row,
                  pl.BlockSpec((tr, LANE), lambda i, s1, s2: (i, 0)),
                  pl.BlockSpec((1, d), lambda i, s1, s2: (0, 0)),
                  pl.BlockSpec(memory_space=pl.ANY)],
        out_specs=[row, row],
        scratch_shapes=[pltpu.VMEM((2, 2, tr, d), F32), pltpu.SemaphoreType.DMA((2,))],
    )
    return pl.pallas_call(
        functools.partial(_combine_body, tr=tr),
        grid_spec=grid_spec,
        out_shape=[jax.ShapeDtypeStruct((m, d), F32), jax.ShapeDtypeStruct((m, d), norm_dtype)],
        compiler_params=pltpu.CompilerParams(dimension_semantics=("arbitrary",)),
        name="combine_rows",
    )(slot1, slot2, h, gates, g_next.reshape(1, d).astype(F32), y)


def moe_ffn(h, g, w_router, w_gate, w_up, w_down, g_next, norm_dtype, *, layer):
    m, _ = h.shape
    tm = min(MOE_TM, m)
    n_tiles = -(-(2 * m) // tm) + N_EXPERTS
    route, counts = router(h, g, w_router)
    col = lambda j: route[j].astype(jnp.int32)
    cnt = counts[:, 0].astype(jnp.int32)
    tiles_per = (cnt + tm - 1) // tm
    tile_end = jnp.cumsum(tiles_per)
    tile_start = tile_end - tiles_per
    group_start = tile_start * tm
    slot1 = group_start[col(ROUTE_I1)] + col(ROUTE_P1)
    slot2 = group_start[col(ROUTE_I2)] + col(ROUTE_P2)
    tile_ids = jnp.arange(n_tiles)
    tile_expert = jnp.minimum(jnp.sum(tile_ids[:, None] >= tile_end[None, :], axis=1), N_EXPERTS - 1)
    tile_rows = jnp.clip(cnt[tile_expert] - (tile_ids - tile_start[tile_expert]) * tm, 0, tm)
    meta = jnp.concatenate([tile_expert, tile_end[-1:], tile_rows]).astype(jnp.int32)
    xs = scatter_rows(h, g, slot1, slot2, n_tiles * tm)
    ys = grouped_ffn(xs, meta, w_gate, w_up, w_down, layer=layer, tm=tm)
    gates = jnp.zeros((m, LANE), F32).at[:, :2].set(route[ROUTE_G1:ROUTE_G2 + 1].T)
    return combine_rows(h, gates, ys, slot1, slot2, g_next, norm_dtype)


def prep_even(p):
    wide = N_HEADS * HEAD_DIM
    w_in = p["w_in"]
    n_layers, d, _ = w_in.shape
    w_fg = jnp.zeros((n_layers, d, LANE), BF16).at[:, :, :N_HEADS].set(w_in[:, :, 6 * wide:].astype(BF16))
    return dict(p, w_qkv=w_in[:, :, :6 * wide].astype(BF16), w_fg=w_fg, w_o=p["w_o"].astype(BF16),
                w_gate=p["w_gate"].astype(BF16), w_up=p["w_up"].astype(BF16),
                w_down=p["w_down"].astype(BF16))


def even_layer(h, hn, p, i, g_next, *, batch, seq):
    qkv = matmul([hn], [(p["w_qkv"], 0)], BF16, layer=i)
    fg = matmul([hn], [(p["w_fg"], 0)], F32, layer=i)
    c = decay_cumsum(fg, p["b_f"][i], batch, seq)
    c_row = c[:, :N_HEADS].reshape(batch, seq, N_HEADS).transpose(0, 2, 1).reshape(batch, N_HEADS, 1, seq)
    o_a = flash_attention(qkv, qkv, qkv, batch=batch, seq=seq, dqk=HEAD_DIM,
                          q_col=lambda hd: hd, k_col=lambda hd: N_HEADS + hd,
                          v_col=lambda hd: 2 * N_HEADS + hd, scale=HEAD_DIM ** -0.5,
                          decay=(c, c_row))
    o_b = stick_breaking_attention(qkv, batch=batch, seq=seq, q_col=lambda hd: 3 * N_HEADS + hd,
                                   k_col=lambda hd: 4 * N_HEADS + hd,
                                   v_col=lambda hd: 5 * N_HEADS + hd)
    h, hn = out_proj_norm([o_a, o_b], p["w_o"], h, p["norm_ffn"][i], layer=i)
    return dense_ffn(hn, p["w_gate"], p["w_up"], p["w_down"], h, g_next, layer=i)


def prep_odd(p):
    w_in = p["w_in"]
    n_layers, d, _ = w_in.shape
    lat_w = Q_LORA + KV_LORA
    w_lat = jnp.zeros((n_layers, d, lat_w + LANE), BF16).at[:, :, :lat_w + ROPE_DIM].set(
        w_in[:, :, :lat_w + ROPE_DIM].astype(BF16))
    w_uq = jnp.zeros((n_layers, Q_LORA, N_HEADS, MLA_QK), BF16).at[:, :, :, :HEAD_DIM + ROPE_DIM].set(
        p["w_uq"].reshape(n_layers, Q_LORA, N_HEADS, HEAD_DIM + ROPE_DIM).astype(BF16))
    return dict(p, w_lat=w_lat, w_dil=w_in[:, :, lat_w + ROPE_DIM:].astype(BF16),
                w_uq=w_uq.reshape(n_layers, Q_LORA, N_HEADS * MLA_QK),
                w_ukv=p["w_ukv"].astype(BF16), w_o=p["w_o"].astype(BF16))


def odd_layer(h, hn, p, i, tables, g_next, norm_dtype, *, batch, seq):
    lat = matmul([hn], [(p["w_lat"], 0)], F32, layer=i)
    dqkv = dilated_proj(hn, p["w_dil"], layer=i)
    cqn = rmsnorm(lat, p["g_cq"][i], BF16, col_block=0, width=Q_LORA)
    ckvn = rmsnorm(lat, p["g_ckv"][i], BF16, col_block=1, width=KV_LORA)
    q_raw = matmul([cqn], [(p["w_uq"], 0)], F32, layer=i)
    kv = matmul([ckvn], [(p["w_ukv"], 0)], BF16, layer=i)
    q_full, k_full = mla_prep(q_raw, lat, kv, tables, seq=seq)
    o_c = flash_attention(q_full, k_full, kv, batch=batch, seq=seq, dqk=MLA_QK,
                          q_col=lambda hd: hd, k_col=lambda hd: hd, v_col=lambda hd: 2 * hd + 1,
                          scale=MLA_SCALE_DIM ** -0.5)
    branches = [dilated_branch(x, batch=batch, seq=seq, rate=r) for x, r in zip(dqkv, DIL_RATES)]
    o_d = dilated_merge([o for o, _ in branches], [l for _, l in branches])
    h = matmul([o_c, o_d], [(p["w_o"], 0), (p["w_o"], 1)], F32, res=h, layer=i)
    return moe_ffn(h, p["norm_ffn"][i], p["w_router"][i], p["w_exp_gate"], p["w_exp_up"],
                   p["w_exp_down"], g_next, norm_dtype, layer=i)


def kernel(x, even_norm_mix, even_w_in, even_b_f, even_w_o, even_norm_ffn, even_w_gate, even_w_up,
           even_w_down, odd_norm_mix, odd_w_in, odd_g_cq, odd_g_ckv, odd_w_uq, odd_w_ukv, odd_w_o,
           odd_norm_ffn, odd_w_router, odd_w_exp_gate, odd_w_exp_up, odd_w_exp_down, final_norm):
    batch, seq, d = x.shape
    assert seq % (max(DIL_RATES) * DIL_BLOCK) == 0
    even = prep_even(dict(norm_mix=even_norm_mix, w_in=even_w_in, b_f=even_b_f, w_o=even_w_o,
                          norm_ffn=even_norm_ffn, w_gate=even_w_gate, w_up=even_w_up,
                          w_down=even_w_down))
    odd = prep_odd(dict(norm_mix=odd_norm_mix, w_in=odd_w_in, g_cq=odd_g_cq, g_ckv=odd_g_ckv,
                        w_uq=odd_w_uq, w_ukv=odd_w_ukv, w_o=odd_w_o, norm_ffn=odd_norm_ffn,
                        w_router=odd_w_router, w_exp_gate=odd_w_exp_gate, w_exp_up=odd_w_exp_up,
                        w_exp_down=odd_w_exp_down))
    depth = even_w_in.shape[0] + odd_w_in.shape[0]
    tables = rope_tables(seq)
    assert depth % 2 == 0, "the final norm is fused into the last odd layer"
    h = x.reshape(batch * seq, d)
    hn = rmsnorm(h, even_norm_mix[0], BF16)
    for layer in range(depth):
        i = layer // 2
        if layer % 2 == 0:
            h, hn = even_layer(h, hn, even, i, odd_norm_mix[i], batch=batch, seq=seq)
        elif layer + 1 < depth:
            h, hn = odd_layer(h, hn, odd, i, tables, even_norm_mix[i + 1], BF16, batch=batch, seq=seq)
        else:
            h, hn = odd_layer(h, hn, odd, i, tables, final_norm, x.dtype, batch=batch, seq=seq)
    return hn.reshape(batch, seq, d)
```

```python
import functools

import jax
import jax.numpy as jnp
from jax import lax
from jax.experimental import pallas as pl
from jax.experimental.pallas import tpu as pltpu

F32 = jnp.float32
BF16 = jnp.bfloat16

LANE = 128
SUBLANE = 8
HEAD_DIM = 128
N_HEADS = 8
Q_LORA = 512
KV_LORA = 512
ROPE_DIM = 64
MLA_QK = 256
MLA_SCALE_DIM = 192
ROPE_THETA = 10000.0
DIL_RATES = (1, 4, 16)
DIL_BLOCK = 128
N_EXPERTS = 8
NORM_EPS = 1e-6
NEG_INF = float("-inf")
LOG2E = 1.4426950408889634

ROW_TILE = 512
MM_TILE = 1024
ATT_TQ = 512
ATT_TK = 512
SB_TK = 256
SB_DEAD_LOG2 = -160.0
FFN_TM = 512
FFN_TF = 512
MOE_TM = 1024
MOE_SUB = 512
MOE_TF = 512
DMA_ROWS = 256
DMA_UNROLL = 4


def _rms(x, g):
    ms = jnp.mean(x * x, axis=-1, keepdims=True)
    return x * lax.rsqrt(ms + NORM_EPS) * g


def _log_sigmoid(x):
    return -(jnp.maximum(-x, 0.0) + jnp.log1p(jnp.exp(-jnp.abs(x))))


def _rmsnorm_body(x_ref, g_ref, o_ref):
    o_ref[...] = _rms(x_ref[...].astype(F32), g_ref[...]).astype(o_ref.dtype)


def rmsnorm(x, g, out_dtype, *, col_block=0, width=None):
    m = x.shape[0]
    width = x.shape[1] if width is None else width
    tm = min(ROW_TILE, m)
    return pl.pallas_call(
        _rmsnorm_body,
        grid=(m // tm,),
        in_specs=[pl.BlockSpec((tm, width), lambda i: (i, col_block)),
                  pl.BlockSpec((1, width), lambda i: (0, 0))],
        out_specs=pl.BlockSpec((tm, width), lambda i: (i, 0)),
        out_shape=jax.ShapeDtypeStruct((m, width), out_dtype),
        name="rmsnorm",
    )(x, g.reshape(1, width).astype(F32))


def _matmul_body(*refs, n_in, has_res):
    o_ref = refs[-1]
    acc = jnp.dot(refs[0][...], refs[n_in][...], preferred_element_type=F32)
    for i in range(1, n_in):
        acc = acc + jnp.dot(refs[i][...], refs[n_in + i][...], preferred_element_type=F32)
    if has_res:
        acc = refs[2 * n_in][...] + acc
    o_ref[...] = acc.astype(o_ref.dtype)


def matmul(a_list, w_list, out_dtype, res=None, *, layer=0, n=None):
    m = a_list[0].shape[0]
    n = w_list[0][0].shape[2] if n is None else n
    tm = min(MM_TILE, m)
    tn = MM_TILE if n % MM_TILE == 0 else n

    def w_spec(a, row_block):
        return pl.BlockSpec((None, a.shape[1], tn), lambda i, j: (layer, row_block, j))

    in_specs = [pl.BlockSpec((tm, a.shape[1]), lambda i, j: (i, 0)) for a in a_list]
    in_specs += [w_spec(a, rb) for a, (_, rb) in zip(a_list, w_list)]
    args = list(a_list) + [w for w, _ in w_list]
    if res is not None:
        in_specs.append(pl.BlockSpec((tm, tn), lambda i, j: (i, j)))
        args.append(res)
    return pl.pallas_call(
        functools.partial(_matmul_body, n_in=len(a_list), has_res=res is not None),
        grid=(m // tm, n // tn),
        in_specs=in_specs,
        out_specs=pl.BlockSpec((tm, tn), lambda i, j: (i, j)),
        out_shape=jax.ShapeDtypeStruct((m, n), out_dtype),
        name="matmul",
    )(*args)


def _out_proj_norm_body(*refs, n_in):
    res_ref, g_ref, h_ref, hn_ref = refs[2 * n_in:]
    acc = jnp.dot(refs[0][...], refs[n_in][...], preferred_element_type=F32)
    for i in range(1, n_in):
        acc = acc + jnp.dot(refs[i][...], refs[n_in + i][...], preferred_element_type=F32)
    h = res_ref[...] + acc
    h_ref[...] = h
    hn_ref[...] = _rms(h, g_ref[...]).astype(hn_ref.dtype)


def out_proj_norm(a_list, w, res, g, *, layer):
    m, d = res.shape
    tm = min(ROW_TILE, m)

    def w_spec(a, row_block):
        return pl.BlockSpec((None, a.shape[1], d), lambda i: (layer, row_block, 0))

    row = pl.BlockSpec((tm, d), lambda i: (i, 0))
    return pl.pallas_call(
        functools.partial(_out_proj_norm_body, n_in=len(a_list)),
        grid=(m // tm,),
        in_specs=[pl.BlockSpec((tm, a.shape[1]), lambda i: (i, 0)) for a in a_list]
        + [w_spec(a, rb) for rb, a in enumerate(a_list)]
        + [row, pl.BlockSpec((1, d), lambda i: (0, 0))],
        out_specs=[row, row],
        out_shape=[jax.ShapeDtypeStruct((m, d), F32), jax.ShapeDtypeStruct((m, d), BF16)],
        name="out_proj_norm",
    )(*a_list, *([w] * len(a_list)), res, g.reshape(1, d).astype(F32))


def _decay_body(fg_ref, b_ref, c_ref, *, seq):
    bias = b_ref[...]
    row = lax.broadcasted_iota(jnp.int32, (SUBLANE, LANE), 0)

    def body(i, carry):
        st = pl.multiple_of(i * SUBLANE, SUBLANE)
        x = _log_sigmoid(fg_ref[pl.ds(st, SUBLANE), :] + bias)
        for s in (1, 2, 4):
            x = x + jnp.where(row >= s, pltpu.roll(x, s, 0), 0.0)
        x = x + carry
        c_ref[pl.ds(st, SUBLANE), :] = x
        return jnp.broadcast_to(x[SUBLANE - 1:SUBLANE, :], (SUBLANE, LANE))

    lax.fori_loop(0, seq // SUBLANE, body, jnp.zeros((SUBLANE, LANE), F32), unroll=8)


def decay_cumsum(fg, b_f, batch, seq):
    bias = jnp.zeros((1, LANE), F32).at[0, :b_f.shape[0]].set(b_f.astype(F32))
    return pl.pallas_call(
        functools.partial(_decay_body, seq=seq),
        grid=(batch,),
        in_specs=[pl.BlockSpec((seq, LANE), lambda b: (b, 0)),
                  pl.BlockSpec((1, LANE), lambda b: (0, 0))],
        out_specs=pl.BlockSpec((seq, LANE), lambda b: (b, 0)),
        out_shape=jax.ShapeDtypeStruct((batch * seq, LANE), F32),
        name="decay_cumsum",
    )(fg, bias)


def _flash_body(*refs, tq, tk, scale, has_decay):
    if has_decay:
        q_ref, k_ref, v_ref, c_ref, crow_ref, o_ref = refs
    else:
        q_ref, k_ref, v_ref, o_ref = refs
    h = pl.program_id(1)
    qi = pl.program_id(2)
    q = q_ref[...]
    to_log2 = scale * LOG2E
    if has_decay:
        cq = crow_ref[:, pl.ds(pl.multiple_of(qi * tq, tq), tq)] * LOG2E

    def span(kb, n, carry, masked):
        m, l, acc = carry
        start = pl.multiple_of(kb * tk, tk)
        k = k_ref[pl.ds(start, n * tk), :]
        t_all = lax.dot_general(k, q, (((1,), (1,)), ((), ())), preferred_element_type=F32) * to_log2
        if has_decay:
            cblk = c_ref[pl.ds(start, n * tk), :]
            lane = lax.broadcasted_iota(jnp.int32, cblk.shape, 1)
            ck = jnp.sum(jnp.where(lane == h, cblk, 0.0), axis=-1, keepdims=True)
            t_all = t_all - ck * LOG2E
        for j in range(n):
            t = t_all[j * tk:(j + 1) * tk, :]
            if masked:
                key = lax.broadcasted_iota(jnp.int32, (tk, tq), 0)
                qry = lax.broadcasted_iota(jnp.int32, (tk, tq), 1)
                t = jnp.where(key <= qry, t, NEG_INF)
            top = jnp.max(t, axis=0, keepdims=True)
            if has_decay:
                top = top + cq
            m_new = jnp.maximum(m, top)
            shift = m_new - cq if has_decay else m_new
            alpha = jnp.exp2(m - m_new)
            p = jnp.exp2(t - shift)
            l = alpha * l + jnp.sum(p, axis=0, keepdims=True)
            v = v_ref[pl.ds(pl.multiple_of(start + j * tk, tk), tk), :]
            pv = lax.dot_general(v, p.astype(BF16), (((0,), (0,)), ((), ())),
                                 preferred_element_type=F32)
            acc = alpha * acc + pv
            m = m_new
        return m, l, acc

    init = (jnp.full((1, tq), NEG_INF, F32), jnp.zeros((1, tq), F32),
            jnp.zeros((v_ref.shape[1], tq), F32))
    carry = lax.fori_loop(0, qi // 4, lambda i, c: span(4 * i, 4, c, False), init)
    carry = lax.cond(qi % 4 >= 2, lambda c: span((qi // 4) * 4, 2, c, False), lambda c: c, carry)
    carry = lax.cond(qi % 2 == 1, lambda c: span(qi - 1, 1, c, False), lambda c: c, carry)
    m, l, acc = span(qi, 1, carry, True)
    o_ref[...] = (acc / l).T.astype(o_ref.dtype)


def flash_attention(q_arr, k_arr, v_arr, *, batch, seq, dqk, q_col, k_col, v_col, scale,
                    decay=None):
    tq = tk = min(ATT_TQ, seq)
    nq = seq // tq
    in_specs = [pl.BlockSpec((tq, dqk), lambda b, h, i: (b * nq + i, q_col(h))),
                pl.BlockSpec((seq, dqk), lambda b, h, i: (b, k_col(h))),
                pl.BlockSpec((seq, HEAD_DIM), lambda b, h, i: (b, v_col(h)))]
    args = [q_arr, k_arr, v_arr]
    if decay is not None:
        in_specs += [pl.BlockSpec((seq, LANE), lambda b, h, i: (b, 0)),
                     pl.BlockSpec((None, None, 1, seq), lambda b, h, i: (b, h, 0, 0))]
        args += list(decay)
    return pl.pallas_call(
        functools.partial(_flash_body, tq=tq, tk=tk, scale=scale, has_decay=decay is not None),
        grid=(batch, N_HEADS, nq),
        in_specs=in_specs,
        out_specs=pl.BlockSpec((tq, HEAD_DIM), lambda b, h, i: (b * nq + i, h)),
        out_shape=jax.ShapeDtypeStruct((batch * seq, N_HEADS * HEAD_DIM), BF16),
        name="flash_attention",
    )(*args)


def _sb_body(q_ref, k_ref, v_ref, o_ref, *, tq, tk, scale):
    qi = pl.program_id(2)
    q = q_ref[...]
    jrow = lax.broadcasted_iota(jnp.int32, (tk, tk), 0)
    scol = lax.broadcasted_iota(jnp.int32, (tk, tk), 1)
    later = jnp.where(jrow > scol, 1.0, 0.0).astype(BF16)
    later2 = jnp.concatenate([later, later], axis=0)

    def span(first_block, n_blocks, carry, masked):
        run, acc = carry
        width = n_blocks * tk
        base = pl.multiple_of(first_block * tk, tk)
        k = k_ref[pl.ds(base, width), :]
        v = v_ref[pl.ds(base, width), :]
        z = lax.dot_general(q, k, (((1,), (1,)), ((), ())), preferred_element_type=F32) * (scale * LOG2E)
        sp = jnp.maximum(z, 0.0) + jnp.log2(1.0 + jnp.exp2(-jnp.abs(z)))
        l1m = -sp
        if masked:
            strict = (lax.broadcasted_iota(jnp.int32, (tq, width), 1)
                      < lax.broadcasted_iota(jnp.int32, (tq, width), 0))
            l1m = jnp.where(strict, l1m, 0.0)
        hi = l1m.astype(BF16)
        lo = (l1m - hi.astype(F32)).astype(BF16)
        blocks = [slice(j * tk, (j + 1) * tk) for j in range(width // tk)]
        after = [jnp.dot(jnp.concatenate([hi[:, b], lo[:, b]], axis=1), later2,
                         preferred_element_type=F32) for b in blocks]
        sums = [jnp.sum(l1m[:, b], axis=-1, keepdims=True) for b in blocks]
        logw = []
        for j in reversed(range(len(blocks))):
            logw.append((z[:, blocks[j]] - sp[:, blocks[j]]) + after[j] + run)
            run = run + sums[j]
        a = jnp.exp2(jnp.concatenate(logw[::-1], axis=1))
        if masked:
            a = jnp.where(strict, a, 0.0)
        acc = acc + jnp.dot(a.astype(BF16), v, preferred_element_type=F32)
        return run, acc

    per_q = tq // tk
    carry = (jnp.zeros((tq, 1), F32), jnp.zeros((tq, HEAD_DIM), F32))
    run, acc = span(qi * per_q, per_q, carry, True)

    def live(state):
        block, run, _ = state
        return (block >= 0) & (jnp.max(run) > SB_DEAD_LOG2)

    def earlier(state):
        block, run, acc = state
        run, acc = span(block, 1, (run, acc), False)
        return block - 1, run, acc

    _, _, acc = lax.while_loop(live, earlier, (qi * per_q - 1, run, acc))
    o_ref[...] = acc.astype(o_ref.dtype)


def stick_breaking_attention(qkv, *, batch, seq, q_col, k_col, v_col):
    tq = min(ATT_TQ, seq)
    tk = min(SB_TK, tq)
    nq = seq // tq
    return pl.pallas_call(
        functools.partial(_sb_body, tq=tq, tk=tk, scale=HEAD_DIM ** -0.5),
        grid=(batch, N_HEADS, nq),
        in_specs=[pl.BlockSpec((tq, HEAD_DIM), lambda b, h, i: (b * nq + i, q_col(h))),
                  pl.BlockSpec((seq, HEAD_DIM), lambda b, h, i: (b, k_col(h))),
                  pl.BlockSpec((seq, HEAD_DIM), lambda b, h, i: (b, v_col(h)))],
        out_specs=pl.BlockSpec((tq, HEAD_DIM), lambda b, h, i: (b * nq + i, h)),
        out_shape=jax.ShapeDtypeStruct((batch * seq, N_HEADS * HEAD_DIM), BF16),
        name="stick_breaking",
    )(qkv, qkv, qkv)


def _mla_prep_body(q_ref, kr_ref, kv_ref, cos_ref, sin_lo_ref, sin_hi_ref, qf_ref, kf_ref):
    cos, sin_lo, sin_hi = cos_ref[...], sin_lo_ref[...], sin_hi_ref[...]

    def rope(x):
        return x * cos + pltpu.roll(x, LANE - ROPE_DIM // 2, 1) * sin_lo \
            + pltpu.roll(x, ROPE_DIM // 2, 1) * sin_hi

    k_pe = rope(kr_ref[...]).astype(BF16)
    for h in range(N_HEADS):
        lo = h * MLA_QK
        qf_ref[:, lo:lo + HEAD_DIM] = q_ref[:, lo:lo + HEAD_DIM].astype(BF16)
        qf_ref[:, lo + HEAD_DIM:lo + MLA_QK] = rope(q_ref[:, lo + HEAD_DIM:lo + MLA_QK]).astype(BF16)
        kf_ref[:, lo:lo + HEAD_DIM] = kv_ref[:, lo:lo + HEAD_DIM]
        kf_ref[:, lo + HEAD_DIM:lo + MLA_QK] = k_pe


def mla_prep(q_raw, lat, kv, tables, *, seq):
    m = q_raw.shape[0]
    tm = min(ROW_TILE, seq)
    npos = seq // tm
    wide = N_HEADS * MLA_QK
    kr_block = (Q_LORA + KV_LORA) // LANE
    tab_spec = pl.BlockSpec((tm, LANE), lambda i: (i % npos, 0))
    return pl.pallas_call(
        _mla_prep_body,
        grid=(m // tm,),
        in_specs=[pl.BlockSpec((tm, wide), lambda i: (i, 0)),
                  pl.BlockSpec((tm, LANE), lambda i: (i, kr_block)),
                  pl.BlockSpec((tm, wide), lambda i: (i, 0)),
                  tab_spec, tab_spec, tab_spec],
        out_specs=[pl.BlockSpec((tm, wide), lambda i: (i, 0)),
                   pl.BlockSpec((tm, wide), lambda i: (i, 0))],
        out_shape=[jax.ShapeDtypeStruct((m, wide), BF16),
                   jax.ShapeDtypeStruct((m, wide), BF16)],
        name="mla_prep",
    )(q_raw, lat, kv, *tables)


def rope_tables(seq):
    half = ROPE_DIM // 2
    inv_freq = ROPE_THETA ** (-jnp.arange(half, dtype=F32) / half)
    ang = jnp.arange(seq, dtype=F32)[:, None] * inv_freq[None, :]
    cos, sin = jnp.cos(ang), jnp.sin(ang)
    zeros = jnp.zeros((seq, half), F32)
    pad = jnp.zeros((seq, LANE - ROPE_DIM), F32)
    cos_t = jnp.concatenate([cos, cos, pad], axis=1)
    sin_lo = jnp.concatenate([-sin, zeros, pad], axis=1)
    sin_hi = jnp.concatenate([zeros, sin, pad], axis=1)
    return cos_t, sin_lo, sin_hi


def _dilated_body(q_ref, kp_ref, kc_ref, vp_ref, vc_ref, o_ref, lse_ref, *, rate, scale):
    ib = pl.program_id(2)
    n = DIL_BLOCK
    qi = lax.broadcasted_iota(jnp.int32, (n, 2 * n), 0)
    ki = lax.broadcasted_iota(jnp.int32, (n, 2 * n), 1)
    steps = n + qi - ki
    valid = (steps >= 0) & (steps <= n) & ((ki >= n) | (ib > 0))
    dist = (steps * rate).astype(F32)
    lane = lax.broadcasted_iota(jnp.int32, (n, LANE), 1)
    lse_all = jnp.zeros((n, LANE), F32)
    for h in range(N_HEADS):
        cols = slice(h * HEAD_DIM, (h + 1) * HEAD_DIM)
        q = q_ref[:, cols]
        k = jnp.concatenate([kp_ref[:, cols], kc_ref[:, cols]], axis=0)
        v = jnp.concatenate([vp_ref[:, cols], vc_ref[:, cols]], axis=0)
        s = lax.dot_general(q, k, (((1,), (1,)), ((), ())), preferred_element_type=F32) * scale
        s = jnp.where(valid, s - (2.0 ** -(h + 1)) * dist, NEG_INF)
        m = jnp.max(s, axis=-1, keepdims=True)
        p = jnp.exp(s - m)
        l = jnp.sum(p, axis=-1, keepdims=True)
        o = jnp.dot(p.astype(BF16), v, preferred_element_type=F32) / l
        o_ref[:, cols] = o.astype(o_ref.dtype)
        lse_all = jnp.where(lane == h, m + jnp.log(l), lse_all)
    lse_ref[...] = lse_all


def _dilated_proj_body(a_ref, w_ref, *refs):
    out_refs, acc_ref = refs[:-1], refs[-1]
    acc = jnp.dot(a_ref[...], w_ref[...], preferred_element_type=F32)
    n_slabs, tm, _ = acc_ref.shape
    for j in range(n_slabs):
        acc_ref[j] = acc[:, j * LANE:(j + 1) * LANE]
    for rate, o_ref in zip(DIL_RATES, out_refs):
        if rate == 1:
            o_ref[...] = acc.astype(o_ref.dtype)
            continue
        for c in range(rate):
            for j in range(n_slabs):
                o_ref[c, :, j * LANE:(j + 1) * LANE] = (
                    acc_ref[j, pl.ds(c, tm // rate, stride=rate), :].astype(o_ref.dtype))


def dilated_proj(a, w, *, layer):
    m, k = a.shape
    n = w.shape[2]
    tm, tn = min(MM_TILE, m), MM_TILE

    def out_spec(rate):
        if rate == 1:
            return pl.BlockSpec((tm, tn), lambda i, j: (i, j))
        return pl.BlockSpec((rate, tm // rate, tn), lambda i, j: (0, i, j))

    def out_shape(rate):
        return jax.ShapeDtypeStruct((m, n) if rate == 1 else (rate, m // rate, n), BF16)

    return pl.pallas_call(
        _dilated_proj_body,
        grid=(m // tm, n // tn),
        in_specs=[pl.BlockSpec((tm, k), lambda i, j: (i, 0)),
                  pl.BlockSpec((None, k, tn), lambda i, j: (layer, 0, j))],
        out_specs=[out_spec(r) for r in DIL_RATES],
        out_shape=[out_shape(r) for r in DIL_RATES],
        scratch_shapes=[pltpu.VMEM((tn // LANE, tm, LANE), F32)],
        name="dilated_proj",
    )(a, w)


def dilated_branch(dqkv, *, batch, seq, rate):
    wide = N_HEADS * HEAD_DIM
    n_rows = batch * seq
    nb = seq // rate // DIL_BLOCK
    rows = dqkv.reshape(n_rows, 3 * wide)

    def row_block(b, c, i):
        return (c * batch + b) * nb + i

    def spec(which, prev):
        def index(b, c, i):
            return (row_block(b, c, jnp.maximum(i - 1, 0) if prev else i), which)
        return pl.BlockSpec((DIL_BLOCK, wide), index)

    o, lse = pl.pallas_call(
        functools.partial(_dilated_body, rate=rate, scale=HEAD_DIM ** -0.5),
        grid=(batch, rate, nb),
        in_specs=[spec(0, False), spec(1, True), spec(1, False), spec(2, True), spec(2, False)],
        out_specs=[pl.BlockSpec((DIL_BLOCK, wide), lambda b, c, i: (row_block(b, c, i), 0)),
                   pl.BlockSpec((DIL_BLOCK, LANE), lambda b, c, i: (row_block(b, c, i), 0))],
        out_shape=[jax.ShapeDtypeStruct((n_rows, wide), BF16),
                   jax.ShapeDtypeStruct((n_rows, LANE), F32)],
        name="dilated_branch",
    )(rows, rows, rows, rows, rows)
    if rate == 1:
        return o, lse
    return o.reshape(rate, n_rows // rate, wide), lse.reshape(rate, n_rows // rate, LANE)


def _dilated_merge_body(*refs):
    n = len(DIL_RATES)
    o_refs, l_refs, out_ref = refs[:n], refs[n:2 * n], refs[2 * n]
    scratch = refs[2 * n + 1:]
    outs, lses = [], []
    for g, rate in enumerate(DIL_RATES):
        if rate == 1:
            outs.append(o_refs[g][...].astype(F32))
            lses.append(l_refs[g][...])
            continue
        o_nat, l_nat = scratch[2 * (g - 1)], scratch[2 * (g - 1) + 1]
        rows = l_nat.shape[0] // rate
        for c in range(rate):
            for h in range(N_HEADS):
                o_nat[h, pl.ds(c, rows, stride=rate), :] = (
                    o_refs[g][c, :, h * HEAD_DIM:(h + 1) * HEAD_DIM].astype(F32))
            l_nat[pl.ds(c, rows, stride=rate), :] = l_refs[g][c]
        outs.append(o_nat)
        lses.append(l_nat[...])
    top = functools.reduce(jnp.maximum, lses)
    es = [jnp.exp(x - top) for x in lses]
    den = functools.reduce(lambda a, b: a + b, es)
    ws = [e / den for e in es]
    for h in range(N_HEADS):
        cols = slice(h * HEAD_DIM, (h + 1) * HEAD_DIM)
        acc = ws[0][:, h:h + 1] * outs[0][:, cols]
        for g in range(1, n):
            acc = acc + ws[g][:, h:h + 1] * outs[g][h]
        out_ref[:, cols] = acc.astype(out_ref.dtype)


def dilated_merge(outs, lses):
    m = outs[0].shape[0]
    wide = N_HEADS * HEAD_DIM
    tm = min(ROW_TILE, m)

    def spec(rate, width):
        if rate == 1:
            return pl.BlockSpec((tm, width), lambda i: (i, 0))
        return pl.BlockSpec((rate, tm // rate, width), lambda i: (0, i, 0))

    scratch = []
    for rate in DIL_RATES[1:]:
        scratch += [pltpu.VMEM((N_HEADS, tm, HEAD_DIM), F32), pltpu.VMEM((tm, LANE), F32)]
    return pl.pallas_call(
        _dilated_merge_body,
        grid=(m // tm,),
        in_specs=[spec(r, wide) for r in DIL_RATES] + [spec(r, LANE) for r in DIL_RATES],
        out_specs=pl.BlockSpec((tm, wide), lambda i: (i, 0)),
        out_shape=jax.ShapeDtypeStruct((m, wide), BF16),
        scratch_shapes=scratch,
        name="dilated_merge",
    )(*outs, *lses)


def _swiglu_partial(x, wg, wu, wd):
    g = jnp.dot(x, wg, preferred_element_type=F32)
    u = jnp.dot(x, wu, preferred_element_type=F32)
    mid = (g * (1.0 / (1.0 + jnp.exp(-g))) * u).astype(BF16)
    return jnp.dot(mid, wd, preferred_element_type=F32)


def _dense_ffn_body(x_ref, wg_ref, wu_ref, wd_ref, res_ref, g_ref, o_ref, on_ref):
    f = pl.program_id(1)

    @pl.when(f == 0)
    def _():
        o_ref[...] = res_ref[...]

    o_ref[...] += _swiglu_partial(x_ref[...], wg_ref[...], wu_ref[...], wd_ref[...])

    @pl.when(f == pl.num_programs(1) - 1)
    def _():
        on_ref[...] = _rms(o_ref[...], g_ref[...]).astype(on_ref.dtype)


def dense_ffn(x, w_gate, w_up, w_down, res, g_next, *, layer):
    m, d = x.shape
    ff = w_gate.shape[2]
    tm, tf = min(FFN_TM, m), min(FFN_TF, ff)
    row = pl.BlockSpec((tm, d), lambda i, f: (i, 0))
    return pl.pallas_call(
        _dense_ffn_body,
        grid=(m // tm, ff // tf),
        in_specs=[row,
                  pl.BlockSpec((None, d, tf), lambda i, f: (layer, 0, f)),
                  pl.BlockSpec((None, d, tf), lambda i, f: (layer, 0, f)),
                  pl.BlockSpec((None, tf, d), lambda i, f: (layer, f, 0)),
                  row,
                  pl.BlockSpec((1, d), lambda i, f: (0, 0))],
        out_specs=[row, row],
        out_shape=[jax.ShapeDtypeStruct((m, d), F32), jax.ShapeDtypeStruct((m, d), BF16)],
        name="dense_ffn",
    )(x, w_gate, w_up, w_down, res, g_next.reshape(1, d).astype(F32))


def _pack_bf16_pairs(lo, hi):
    lo_bits = lax.bitcast_convert_type(lo.astype(BF16).astype(F32), jnp.uint32)
    hi_bits = lax.bitcast_convert_type(hi.astype(BF16).astype(F32), jnp.uint32)
    return (lo_bits >> 16) | (hi_bits & jnp.uint32(0xFFFF0000))


def _unpack_bf16_pairs(packed):
    lo = lax.bitcast_convert_type(packed << 16, F32).astype(BF16)
    hi = lax.bitcast_convert_type(packed & jnp.uint32(0xFFFF0000), F32).astype(BF16)
    return lo, hi


def _grouped_ffn_body(meta_ref, x_ref, wg_ref, wu_ref, wd_ref, o_ref, xb_ref, *, n_tiles, sub):
    i = pl.program_id(0)
    f = pl.program_id(1)
    in_use = i < meta_ref[n_tiles]
    rows = meta_ref[n_tiles + 1 + i]
    half = x_ref.shape[1]

    @pl.when(f == 0)
    def _():
        o_ref[...] = jnp.zeros_like(o_ref)

    @pl.when(in_use & (f == 0))
    def _():
        lo, hi = _unpack_bf16_pairs(x_ref[...])
        xb_ref[:, :half] = lo
        xb_ref[:, half:] = hi

    tm = x_ref.shape[0]
    half_sub = sub // 2

    def run_blocks(blocks):
        wg, wu, wd = (w[...].astype(BF16) for w in (wg_ref, wu_ref, wd_ref))
        for start, size in blocks:
            sl = pl.ds(start, size)
            o_ref[sl, :] += _swiglu_partial(xb_ref[sl, :], wg, wu, wd)

    for n_half in range(1, tm // half_sub + 1):
        full, tail = divmod(n_half, 2)
        blocks = [(b * sub, sub) for b in range(full)] + [(full * sub, half_sub)] * tail
        lo, hi = (n_half - 1) * half_sub, n_half * half_sub
        pl.when(in_use & (rows > lo) & (rows <= hi))(functools.partial(run_blocks, blocks))


def grouped_ffn(xs, meta, w_gate, w_up, w_down, *, layer, tm):
    ns, half = xs.shape
    d = 2 * half
    n_tiles = ns // tm
    ff = w_gate.shape[3]
    tf = min(MOE_TF, ff)
    nf = ff // tf

    def tile(i, meta):
        return jnp.maximum(jnp.minimum(i, meta[n_tiles] - 1), 0)

    def expert(i, meta):
        return meta[tile(i, meta)]

    def chunk(i, f, meta):
        return jnp.where(i < meta[n_tiles], f, nf - 1)

    grid_spec = pltpu.PrefetchScalarGridSpec(
        num_scalar_prefetch=1,
        grid=(n_tiles, nf),
        in_specs=[pl.BlockSpec((tm, half), lambda i, f, meta: (tile(i, meta), 0),
                               pipeline_mode=pl.Buffered(1)),
                  pl.BlockSpec((None, None, d, tf),
                               lambda i, f, meta: (layer, expert(i, meta), 0, chunk(i, f, meta))),
                  pl.BlockSpec((None, None, d, tf),
                               lambda i, f, meta: (layer, expert(i, meta), 0, chunk(i, f, meta))),
                  pl.BlockSpec((None, None, tf, d),
                               lambda i, f, meta: (layer, expert(i, meta), chunk(i, f, meta), 0))],
        out_specs=pl.BlockSpec((tm, d), lambda i, f, meta: (i, 0)),
        scratch_shapes=[pltpu.VMEM((tm, d), BF16)],
    )
    return pl.pallas_call(
        functools.partial(_grouped_ffn_body, n_tiles=n_tiles, sub=min(MOE_SUB, tm)),
        grid_spec=grid_spec,
        out_shape=jax.ShapeDtypeStruct((ns, d), F32),
        name="grouped_ffn",
    )(meta, xs, w_gate, w_up, w_down)


ROUTE_I1, ROUTE_I2, ROUTE_G1, ROUTE_G2, ROUTE_P1, ROUTE_P2 = range(6)


def _router_body(h_ref, g_ref, wr_ref, route_ref, cnt_ref, seen_ref, *, tr):
    @pl.when(pl.program_id(0) == 0)
    def _():
        seen_ref[...] = jnp.zeros_like(seen_ref)

    y = _rms(h_ref[...], g_ref[...])
    y_hi = y.astype(BF16)
    y_lo = (y - y_hi.astype(F32)).astype(BF16)
    w = wr_ref[...]
    w_hi = w.astype(BF16)
    w_lo = (w - w_hi.astype(F32)).astype(BF16)

    def dot_nt(a, b):
        return lax.dot_general(a, b, (((1,), (1,)), ((), ())), preferred_element_type=F32)

    logits = dot_nt(w_hi, y_hi) + (dot_nt(w_hi, y_lo) + dot_nt(w_lo, y_hi))
    expert = lax.broadcasted_iota(jnp.int32, (N_EXPERTS, tr), 0)
    m1 = jnp.max(logits, axis=0, keepdims=True)
    i1 = jnp.min(jnp.where(logits == m1, expert, N_EXPERTS), axis=0, keepdims=True)
    rest = jnp.where(expert == i1, NEG_INF, logits)
    m2 = jnp.max(rest, axis=0, keepdims=True)
    i2 = jnp.min(jnp.where(rest == m2, expert, N_EXPERTS), axis=0, keepdims=True)
    e = jnp.exp(m2 - m1)
    g1 = 1.0 / (1.0 + e)
    g2 = e / (1.0 + e)
    sel = jnp.where((expert == i1) | (expert == i2), 1.0, 0.0)
    r = lax.broadcasted_iota(jnp.int32, (tr, tr), 0)
    c = lax.broadcasted_iota(jnp.int32, (tr, tr), 1)
    earlier = jnp.where(r < c, 1.0, 0.0).astype(BF16)
    seen = seen_ref[...]
    rank = jnp.dot(sel.astype(BF16), earlier, preferred_element_type=F32) + seen[:, 0:1]
    p1 = jnp.sum(jnp.where(expert == i1, rank, 0.0), axis=0, keepdims=True)
    p2 = jnp.sum(jnp.where(expert == i2, rank, 0.0), axis=0, keepdims=True)
    seen = seen + jnp.sum(sel, axis=1, keepdims=True)
    seen_ref[...] = seen
    cnt_ref[...] = seen
    route = jnp.zeros((SUBLANE, tr), F32)
    row = lax.broadcasted_iota(jnp.int32, (SUBLANE, tr), 0)
    for idx, val in ((ROUTE_I1, i1.astype(F32)), (ROUTE_I2, i2.astype(F32)), (ROUTE_G1, g1),
                     (ROUTE_G2, g2), (ROUTE_P1, p1), (ROUTE_P2, p2)):
        route = jnp.where(row == idx, val, route)
    route_ref[...] = route


def router(h, g, w_router):
    assert N_EXPERTS == SUBLANE
    m, d = h.shape
    tr = min(ROW_TILE, m)
    return pl.pallas_call(
        functools.partial(_router_body, tr=tr),
        grid=(m // tr,),
        in_specs=[pl.BlockSpec((tr, d), lambda i: (i, 0)),
                  pl.BlockSpec((1, d), lambda i: (0, 0)),
                  pl.BlockSpec((N_EXPERTS, d), lambda i: (0, 0))],
        out_specs=[pl.BlockSpec((SUBLANE, tr), lambda i: (0, i)),
                   pl.BlockSpec((N_EXPERTS, LANE), lambda i: (0, 0))],
        out_shape=[jax.ShapeDtypeStruct((SUBLANE, m), F32),
                   jax.ShapeDtypeStruct((N_EXPERTS, LANE), F32)],
        scratch_shapes=[pltpu.VMEM((N_EXPERTS, LANE), F32)],
        name="router",
    )(h, g.reshape(1, d).astype(F32), w_router.astype(F32).T)


def _row_copies(copies, step, slot, tr, wait):
    def body(r, _):
        for queue, cp in enumerate(copies(step, slot, r)):
            if wait:
                cp.wait()
            else:
                cp.start(priority=queue)
        return 0

    lax.fori_loop(0, tr, body, 0, unroll=DMA_UNROLL)


def _scatter_body(s1_ref, s2_ref, h_ref, g_ref, xs_in_ref, xs_ref, buf_ref, sems, *, tr):
    del xs_in_ref
    i = pl.program_id(0)
    slot = i % 2
    y = _rms(h_ref[...], g_ref[...])
    half = y.shape[1] // 2
    buf_ref[slot] = _pack_bf16_pairs(y[:, :half], y[:, half:])

    def copies(step, slot, r):
        src = buf_ref.at[slot, pl.ds(r, 1)]
        t = step * tr + r
        return (pltpu.make_async_copy(src, xs_ref.at[pl.ds(s1_ref[t], 1)], sems.at[slot]),
                pltpu.make_async_copy(src, xs_ref.at[pl.ds(s2_ref[t], 1)], sems.at[slot]))

    _row_copies(copies, i, slot, tr, wait=False)

    @pl.when(i > 0)
    def _():
        _row_copies(copies, i - 1, 1 - slot, tr, wait=True)

    @pl.when(i == pl.num_programs(0) - 1)
    def _():
        _row_copies(copies, i, slot, tr, wait=True)


def scatter_rows(h, g, slot1, slot2, n_slots):
    m, d = h.shape
    tr = min(DMA_ROWS, m)
    grid_spec = pltpu.PrefetchScalarGridSpec(
        num_scalar_prefetch=2,
        grid=(m // tr,),
        in_specs=[pl.BlockSpec((tr, d), lambda i, s1, s2: (i, 0)),
                  pl.BlockSpec((1, d), lambda i, s1, s2: (0, 0)),
                  pl.BlockSpec(memory_space=pl.ANY)],
        out_specs=pl.BlockSpec(memory_space=pl.ANY),
        scratch_shapes=[pltpu.VMEM((2, tr, d // 2), jnp.uint32), pltpu.SemaphoreType.DMA((2,))],
    )
    return pl.pallas_call(
        functools.partial(_scatter_body, tr=tr),
        grid_spec=grid_spec,
        out_shape=jax.ShapeDtypeStruct((n_slots, d // 2), jnp.uint32),
        input_output_aliases={4: 0},
        compiler_params=pltpu.CompilerParams(dimension_semantics=("arbitrary",)),
        name="scatter_rows",
    )(slot1, slot2, h, g.reshape(1, d).astype(F32), jnp.zeros((n_slots, d // 2), jnp.uint32))


def _combine_body(s1_ref, s2_ref, h_ref, gate_ref, g_ref, y_ref, o_ref, on_ref, buf_ref, sems, *, tr):
    i = pl.program_id(0)
    slot = i % 2

    def copies(step, slot, r):
        t = step * tr + r
        return (pltpu.make_async_copy(y_ref.at[pl.ds(s1_ref[t], 1)], buf_ref.at[slot, 0, pl.ds(r, 1)],
                                      sems.at[slot]),
                pltpu.make_async_copy(y_ref.at[pl.ds(s2_ref[t], 1)], buf_ref.at[slot, 1, pl.ds(r, 1)],
                                      sems.at[slot]))

    @pl.when(i == 0)
    def _():
        _row_copies(copies, 0, 0, tr, wait=False)

    @pl.when(i + 1 < pl.num_programs(0))
    def _():
        _row_copies(copies, i + 1, 1 - slot, tr, wait=False)

    _row_copies(copies, i, slot, tr, wait=True)
    gates = gate_ref[...]
    h = h_ref[...] + (gates[:, 0:1] * buf_ref[slot, 0] + gates[:, 1:2] * buf_ref[slot, 1])
    o_ref[...] = h
    on_ref[...] = _rms(h, g_ref[...]).astype(on_ref.dtype)


def combine_rows(h, gates, y, slot1, slot2, g_next, norm_dtype):
    m, d = h.shape
    tr = min(DMA_ROWS, m)
    row = pl.BlockSpec((tr, d), lambda i, s1, s2: (i, 0))
    grid_spec = pltpu.PrefetchScalarGridSpec(
        num_scalar_prefetch=2,
        grid=(m // tr,),
        in_specs=[row,
                  pl.BlockSpec((tr, LANE), lambda i, s1, s2: (i, 0)),
                  pl.BlockSpec((1, d), lambda i, s1, s2: (0, 0)),
                  pl.BlockSpec(memory_space=pl.ANY)],
        out_specs=[row, row],
        scratch_shapes=[pltpu.VMEM((2, 2, tr, d), F32), pltpu.SemaphoreType.DMA((2,))],
    )
    return pl.pallas_call(
        functools.partial(_combine_body, tr=tr),
        grid_spec=grid_spec,
        out_shape=[jax.ShapeDtypeStruct((m, d), F32), jax.ShapeDtypeStruct((m, d), norm_dtype)],
        compiler_params=pltpu.CompilerParams(dimension_semantics=("arbitrary",)),
        name="combine_rows",
    )(slot1, slot2, h, gates, g_next.reshape(1, d).astype(F32), y)


def moe_ffn(h, g, w_router, w_gate, w_up, w_down, g_next, norm_dtype, *, layer):
    m, _ = h.shape
    tm = min(MOE_TM, m)
    n_tiles = -(-(2 * m) // tm) + N_EXPERTS
    route, counts = router(h, g, w_router)
    col = lambda j: route[j].astype(jnp.int32)
    cnt = counts[:, 0].astype(jnp.int32)
    tiles_per = (cnt + tm - 1) // tm
    tile_end = jnp.cumsum(tiles_per)
    tile_start = tile_end - tiles_per
    group_start = tile_start * tm
    slot1 = group_start[col(ROUTE_I1)] + col(ROUTE_P1)
    slot2 = group_start[col(ROUTE_I2)] + col(ROUTE_P2)
    tile_ids = jnp.arange(n_tiles)
    tile_expert = jnp.minimum(jnp.sum(tile_ids[:, None] >= tile_end[None, :], axis=1), N_EXPERTS - 1)
    tile_rows = jnp.clip(cnt[tile_expert] - (tile_ids - tile_start[tile_expert]) * tm, 0, tm)
    meta = jnp.concatenate([tile_expert, tile_end[-1:], tile_rows]).astype(jnp.int32)
    xs = scatter_rows(h, g, slot1, slot2, n_tiles * tm)
    ys = grouped_ffn(xs, meta, w_gate, w_up, w_down, layer=layer, tm=tm)
    gates = jnp.zeros((m, LANE), F32).at[:, :2].set(route[ROUTE_G1:ROUTE_G2 + 1].T)
    return combine_rows(h, gates, ys, slot1, slot2, g_next, norm_dtype)


def prep_even(p):
    wide = N_HEADS * HEAD_DIM
    w_in = p["w_in"]
    n_layers, d, _ = w_in.shape
    w_fg = jnp.zeros((n_layers, d, LANE), BF16).at[:, :, :N_HEADS].set(w_in[:, :, 6 * wide:].astype(BF16))
    return dict(p, w_qkv=w_in[:, :, :6 * wide].astype(BF16), w_fg=w_fg, w_o=p["w_o"].astype(BF16),
                w_gate=p["w_gate"].astype(BF16), w_up=p["w_up"].astype(BF16),
                w_down=p["w_down"].astype(BF16))


def even_layer(h, hn, p, i, g_next, *, batch, seq):
    qkv = matmul([hn], [(p["w_qkv"], 0)], BF16, layer=i)
    fg = matmul([hn], [(p["w_fg"], 0)], F32, layer=i)
    c = decay_cumsum(fg, p["b_f"][i], batch, seq)
    c_row = c[:, :N_HEADS].reshape(batch, seq, N_HEADS).transpose(0, 2, 1).reshape(batch, N_HEADS, 1, seq)
    o_a = flash_attention(qkv, qkv, qkv, batch=batch, seq=seq, dqk=HEAD_DIM,
                          q_col=lambda hd: hd, k_col=lambda hd: N_HEADS + hd,
                          v_col=lambda hd: 2 * N_HEADS + hd, scale=HEAD_DIM ** -0.5,
                          decay=(c, c_row))
    o_b = stick_breaking_attention(qkv, batch=batch, seq=seq, q_col=lambda hd: 3 * N_HEADS + hd,
                                   k_col=lambda hd: 4 * N_HEADS + hd,
                                   v_col=lambda hd: 5 * N_HEADS + hd)
    h, hn = out_proj_norm([o_a, o_b], p["w_o"], h, p["norm_ffn"][i], layer=i)
    return dense_ffn(hn, p["w_gate"], p["w_up"], p["w_down"], h, g_next, layer=i)


def prep_odd(p):
    w_in = p["w_in"]
    n_layers, d, _ = w_in.shape
    lat_w = Q_LORA + KV_LORA
    w_lat = jnp.zeros((n_layers, d, lat_w + LANE), BF16).at[:, :, :lat_w + ROPE_DIM].set(
        w_in[:, :, :lat_w + ROPE_DIM].astype(BF16))
    w_uq = jnp.zeros((n_layers, Q_LORA, N_HEADS, MLA_QK), BF16).at[:, :, :, :HEAD_DIM + ROPE_DIM].set(
        p["w_uq"].reshape(n_layers, Q_LORA, N_HEADS, HEAD_DIM + ROPE_DIM).astype(BF16))
    return dict(p, w_lat=w_lat, w_dil=w_in[:, :, lat_w + ROPE_DIM:].astype(BF16),
                w_uq=w_uq.reshape(n_layers, Q_LORA, N_HEADS * MLA_QK),
                w_ukv=p["w_ukv"].astype(BF16), w_o=p["w_o"].astype(BF16))


def odd_layer(h, hn, p, i, tables, g_next, norm_dtype, *, batch, seq):
    lat = matmul([hn], [(p["w_lat"], 0)], F32, layer=i)
    dqkv = dilated_proj(hn, p["w_dil"], layer=i)
    cqn = rmsnorm(lat, p["g_cq"][i], BF16, col_block=0, width=Q_LORA)
    ckvn = rmsnorm(lat, p["g_ckv"][i], BF16, col_block=1, width=KV_LORA)
    q_raw = matmul([cqn], [(p["w_uq"], 0)], F32, layer=i)
    kv = matmul([ckvn], [(p["w_ukv"], 0)], BF16, layer=i)
    q_full, k_full = mla_prep(q_raw, lat, kv, tables, seq=seq)
    o_c = flash_attention(q_full, k_full, kv, batch=batch, seq=seq, dqk=MLA_QK,
                          q_col=lambda hd: hd, k_col=lambda hd: hd, v_col=lambda hd: 2 * hd + 1,
                          scale=MLA_SCALE_DIM ** -0.5)
    branches = [dilated_branch(x, batch=batch, seq=seq, rate=r) for x, r in zip(dqkv, DIL_RATES)]
    o_d = dilated_merge([o for o, _ in branches], [l for _, l in branches])
    h = matmul([o_c, o_d], [(p["w_o"], 0), (p["w_o"], 1)], F32, res=h, layer=i)
    return moe_ffn(h, p["norm_ffn"][i], p["w_router"][i], p["w_exp_gate"], p["w_exp_up"],
                   p["w_exp_down"], g_next, norm_dtype, layer=i)


def kernel(x, even_norm_mix, even_w_in, even_b_f, even_w_o, even_norm_ffn, even_w_gate, even_w_up,
           even_w_down, odd_norm_mix, odd_w_in, odd_g_cq, odd_g_ckv, odd_w_uq, odd_w_ukv, odd_w_o,
           odd_norm_ffn, odd_w_router, odd_w_exp_gate, odd_w_exp_up, odd_w_exp_down, final_norm):
    batch, seq, d = x.shape
    assert seq % (max(DIL_RATES) * DIL_BLOCK) == 0
    even = prep_even(dict(norm_mix=even_norm_mix, w_in=even_w_in, b_f=even_b_f, w_o=even_w_o,
                          norm_ffn=even_norm_ffn, w_gate=even_w_gate, w_up=even_w_up,
                          w_down=even_w_down))
    odd = prep_odd(dict(norm_mix=odd_norm_mix, w_in=odd_w_in, g_cq=odd_g_cq, g_ckv=odd_g_ckv,
                        w_uq=odd_w_uq, w_ukv=odd_w_ukv, w_o=odd_w_o, norm_ffn=odd_norm_ffn,
                        w_router=odd_w_router, w_exp_gate=odd_w_exp_gate, w_exp_up=odd_w_exp_up,
                        w_exp_down=odd_w_exp_down))
    depth = even_w_in.shape[0] + odd_w_in.shape[0]
    tables = rope_tables(seq)
    assert depth % 2 == 0, "the final norm is fused into the last odd layer"
    h = x.reshape(batch * seq, d)
    hn = rmsnorm(h, even_norm_mix[0], BF16)
    for layer in range(depth):
        i = layer // 2
        if layer % 2 == 0:
            h, hn = even_layer(h, hn, even, i, odd_norm_mix[i], batch=batch, seq=seq)
        elif layer + 1 < depth:
            h, hn = odd_layer(h, hn, odd, i, tables, even_norm_mix[i + 1], BF16, batch=batch, seq=seq)
        else:
            h, hn = odd_layer(h, hn, odd, i, tables, final_norm, x.dtype, batch=batch, seq=seq)
    return hn.reshape(batch, seq, d)
```

```python
import functools

import jax
import jax.numpy as jnp
from jax import lax
from jax.experimental import pallas as pl
from jax.experimental.pallas import tpu as pltpu

F32 = jnp.float32
BF16 = jnp.bfloat16

LANE = 128
SUBLANE = 8
HEAD_DIM = 128
N_HEADS = 8
Q_LORA = 512
KV_LORA = 512
ROPE_DIM = 64
MLA_QK = 256
MLA_SCALE_DIM = 192
ROPE_THETA = 10000.0
DIL_RATES = (1, 4, 16)
DIL_BLOCK = 128
N_EXPERTS = 8
NORM_EPS = 1e-6
NEG_INF = float("-inf")
LOG2E = 1.4426950408889634

ROW_TILE = 512
MM_TILE = 1024
ATT_TQ = 512
ATT_TK = 512
SB_TK = 256
SB_DEAD_LOG2 = -160.0
FFN_TM = 512
FFN_TF = 512
MOE_TM = 1024
MOE_SUB = 512
MOE_TF = 256
DMA_ROWS = 256
DMA_UNROLL = 4


def _rms(x, g):
    ms = jnp.mean(x * x, axis=-1, keepdims=True)
    return x * lax.rsqrt(ms + NORM_EPS) * g


def _log_sigmoid(x):
    return -(jnp.maximum(-x, 0.0) + jnp.log1p(jnp.exp(-jnp.abs(x))))


def _rmsnorm_body(x_ref, g_ref, o_ref):
    o_ref[...] = _rms(x_ref[...].astype(F32), g_ref[...]).astype(o_ref.dtype)


def rmsnorm(x, g, out_dtype, *, col_block=0, width=None):
    m = x.shape[0]
    width = x.shape[1] if width is None else width
    tm = min(ROW_TILE, m)
    return pl.pallas_call(
        _rmsnorm_body,
        grid=(m // tm,),
        in_specs=[pl.BlockSpec((tm, width), lambda i: (i, col_block)),
                  pl.BlockSpec((1, width), lambda i: (0, 0))],
        out_specs=pl.BlockSpec((tm, width), lambda i: (i, 0)),
        out_shape=jax.ShapeDtypeStruct((m, width), out_dtype),
        name="rmsnorm",
    )(x, g.reshape(1, width).astype(F32))


def _matmul_body(*refs, n_in, has_res):
    o_ref = refs[-1]
    acc = jnp.dot(refs[0][...], refs[n_in][...], preferred_element_type=F32)
    for i in range(1, n_in):
        acc = acc + jnp.dot(refs[i][...], refs[n_in + i][...], preferred_element_type=F32)
    if has_res:
        acc = refs[2 * n_in][...] + acc
    o_ref[...] = acc.astype(o_ref.dtype)


def matmul(a_list, w_list, out_dtype, res=None, *, layer=0, n=None):
    m = a_list[0].shape[0]
    n = w_list[0][0].shape[2] if n is None else n
    tm = min(MM_TILE, m)
    tn = MM_TILE if n % MM_TILE == 0 else n

    def w_spec(a, row_block):
        return pl.BlockSpec((None, a.shape[1], tn), lambda i, j: (layer, row_block, j))

    in_specs = [pl.BlockSpec((tm, a.shape[1]), lambda i, j: (i, 0)) for a in a_list]
    in_specs += [w_spec(a, rb) for a, (_, rb) in zip(a_list, w_list)]
    args = list(a_list) + [w for w, _ in w_list]
    if res is not None:
        in_specs.append(pl.BlockSpec((tm, tn), lambda i, j: (i, j)))
        args.append(res)
    return pl.pallas_call(
        functools.partial(_matmul_body, n_in=len(a_list), has_res=res is not None),
        grid=(m // tm, n // tn),
        in_specs=in_specs,
        out_specs=pl.BlockSpec((tm, tn), lambda i, j: (i, j)),
        out_shape=jax.ShapeDtypeStruct((m, n), out_dtype),
        name="matmul",
    )(*args)


def _out_proj_norm_body(*refs, n_in):
    res_ref, g_ref, h_ref, hn_ref = refs[2 * n_in:]
    acc = jnp.dot(refs[0][...], refs[n_in][...], preferred_element_type=F32)
    for i in range(1, n_in):
        acc = acc + jnp.dot(refs[i][...], refs[n_in + i][...], preferred_element_type=F32)
    h = res_ref[...] + acc
    h_ref[...] = h
    hn_ref[...] = _rms(h, g_ref[...]).astype(hn_ref.dtype)


def out_proj_norm(a_list, w, res, g, *, layer):
    m, d = res.shape
    tm = min(ROW_TILE, m)

    def w_spec(a, row_block):
        return pl.BlockSpec((None, a.shape[1], d), lambda i: (layer, row_block, 0))

    row = pl.BlockSpec((tm, d), lambda i: (i, 0))
    return pl.pallas_call(
        functools.partial(_out_proj_norm_body, n_in=len(a_list)),
        grid=(m // tm,),
        in_specs=[pl.BlockSpec((tm, a.shape[1]), lambda i: (i, 0)) for a in a_list]
        + [w_spec(a, rb) for rb, a in enumerate(a_list)]
        + [row, pl.BlockSpec((1, d), lambda i: (0, 0))],
        out_specs=[row, row],
        out_shape=[jax.ShapeDtypeStruct((m, d), F32), jax.ShapeDtypeStruct((m, d), BF16)],
        name="out_proj_norm",
    )(*a_list, *([w] * len(a_list)), res, g.reshape(1, d).astype(F32))


def _decay_body(fg_ref, b_ref, c_ref, *, seq):
    bias = b_ref[...]
    row = lax.broadcasted_iota(jnp.int32, (SUBLANE, LANE), 0)

    def body(i, carry):
        st = pl.multiple_of(i * SUBLANE, SUBLANE)
        x = _log_sigmoid(fg_ref[pl.ds(st, SUBLANE), :] + bias)
        for s in (1, 2, 4):
            x = x + jnp.where(row >= s, pltpu.roll(x, s, 0), 0.0)
        x = x + carry
        c_ref[pl.ds(st, SUBLANE), :] = x
        return jnp.broadcast_to(x[SUBLANE - 1:SUBLANE, :], (SUBLANE, LANE))

    lax.fori_loop(0, seq // SUBLANE, body, jnp.zeros((SUBLANE, LANE), F32), unroll=8)


def decay_cumsum(fg, b_f, batch, seq):
    bias = jnp.zeros((1, LANE), F32).at[0, :b_f.shape[0]].set(b_f.astype(F32))
    return pl.pallas_call(
        functools.partial(_decay_body, seq=seq),
        grid=(batch,),
        in_specs=[pl.BlockSpec((seq, LANE), lambda b: (b, 0)),
                  pl.BlockSpec((1, LANE), lambda b: (0, 0))],
        out_specs=pl.BlockSpec((seq, LANE), lambda b: (b, 0)),
        out_shape=jax.ShapeDtypeStruct((batch * seq, LANE), F32),
        name="decay_cumsum",
    )(fg, bias)


def _flash_body(*refs, tq, tk, scale, has_decay):
    if has_decay:
        q_ref, k_ref, v_ref, c_ref, crow_ref, o_ref = refs
    else:
        q_ref, k_ref, v_ref, o_ref = refs
    h = pl.program_id(1)
    qi = pl.program_id(2)
    q = q_ref[...]
    to_log2 = scale * LOG2E
    if has_decay:
        cq = crow_ref[:, pl.ds(pl.multiple_of(qi * tq, tq), tq)] * LOG2E

    def span(kb, n, carry, masked):
        m, l, acc = carry
        start = pl.multiple_of(kb * tk, tk)
        k = k_ref[pl.ds(start, n * tk), :]
        t_all = lax.dot_general(k, q, (((1,), (1,)), ((), ())), preferred_element_type=F32) * to_log2
        if has_decay:
            cblk = c_ref[pl.ds(start, n * tk), :]
            lane = lax.broadcasted_iota(jnp.int32, cblk.shape, 1)
            ck = jnp.sum(jnp.where(lane == h, cblk, 0.0), axis=-1, keepdims=True)
            t_all = t_all - ck * LOG2E
        for j in range(n):
            t = t_all[j * tk:(j + 1) * tk, :]
            if masked:
                key = lax.broadcasted_iota(jnp.int32, (tk, tq), 0)
                qry = lax.broadcasted_iota(jnp.int32, (tk, tq), 1)
                t = jnp.where(key <= qry, t, NEG_INF)
            top = jnp.max(t, axis=0, keepdims=True)
            if has_decay:
                top = top + cq
            m_new = jnp.maximum(m, top)
            shift = m_new - cq if has_decay else m_new
            alpha = jnp.exp2(m - m_new)
            p = jnp.exp2(t - shift)
            l = alpha * l + jnp.sum(p, axis=0, keepdims=True)
            v = v_ref[pl.ds(pl.multiple_of(start + j * tk, tk), tk), :]
            pv = lax.dot_general(v, p.astype(BF16), (((0,), (0,)), ((), ())),
                                 preferred_element_type=F32)
            acc = alpha * acc + pv
            m = m_new
        return m, l, acc

    init = (jnp.full((1, tq), NEG_INF, F32), jnp.zeros((1, tq), F32),
            jnp.zeros((v_ref.shape[1], tq), F32))
    carry = lax.fori_loop(0, qi // 4, lambda i, c: span(4 * i, 4, c, False), init)
    carry = lax.cond(qi % 4 >= 2, lambda c: span((qi // 4) * 4, 2, c, False), lambda c: c, carry)
    carry = lax.cond(qi % 2 == 1, lambda c: span(qi - 1, 1, c, False), lambda c: c, carry)
    m, l, acc = span(qi, 1, carry, True)
    o_ref[...] = (acc / l).T.astype(o_ref.dtype)


def flash_attention(q_arr, k_arr, v_arr, *, batch, seq, dqk, q_col, k_col, v_col, scale,
                    decay=None):
    tq = tk = min(ATT_TQ, seq)
    nq = seq // tq
    in_specs = [pl.BlockSpec((tq, dqk), lambda b, h, i: (b * nq + i, q_col(h))),
                pl.BlockSpec((seq, dqk), lambda b, h, i: (b, k_col(h))),
                pl.BlockSpec((seq, HEAD_DIM), lambda b, h, i: (b, v_col(h)))]
    args = [q_arr, k_arr, v_arr]
    if decay is not None:
        in_specs += [pl.BlockSpec((seq, LANE), lambda b, h, i: (b, 0)),
                     pl.BlockSpec((None, None, 1, seq), lambda b, h, i: (b, h, 0, 0))]
        args += list(decay)
    return pl.pallas_call(
        functools.partial(_flash_body, tq=tq, tk=tk, scale=scale, has_decay=decay is not None),
        grid=(batch, N_HEADS, nq),
        in_specs=in_specs,
        out_specs=pl.BlockSpec((tq, HEAD_DIM), lambda b, h, i: (b * nq + i, h)),
        out_shape=jax.ShapeDtypeStruct((batch * seq, N_HEADS * HEAD_DIM), BF16),
        name="flash_attention",
    )(*args)


def _sb_body(q_ref, k_ref, v_ref, o_ref, *, tq, tk, scale):
    qi = pl.program_id(2)
    q = q_ref[...]
    jrow = lax.broadcasted_iota(jnp.int32, (tk, tk), 0)
    scol = lax.broadcasted_iota(jnp.int32, (tk, tk), 1)
    later = jnp.where(jrow > scol, 1.0, 0.0).astype(BF16)
    later2 = jnp.concatenate([later, later], axis=0)

    def span(first_block, n_blocks, carry, masked):
        run, acc = carry
        width = n_blocks * tk
        base = pl.multiple_of(first_block * tk, tk)
        k = k_ref[pl.ds(base, width), :]
        v = v_ref[pl.ds(base, width), :]
        z = lax.dot_general(q, k, (((1,), (1,)), ((), ())), preferred_element_type=F32) * (scale * LOG2E)
        sp = jnp.maximum(z, 0.0) + jnp.log2(1.0 + jnp.exp2(-jnp.abs(z)))
        l1m = -sp
        if masked:
            strict = (lax.broadcasted_iota(jnp.int32, (tq, width), 1)
                      < lax.broadcasted_iota(jnp.int32, (tq, width), 0))
            l1m = jnp.where(strict, l1m, 0.0)
        hi = l1m.astype(BF16)
        lo = (l1m - hi.astype(F32)).astype(BF16)
        blocks = [slice(j * tk, (j + 1) * tk) for j in range(width // tk)]
        after = [jnp.dot(jnp.concatenate([hi[:, b], lo[:, b]], axis=1), later2,
                         preferred_element_type=F32) for b in blocks]
        sums = [jnp.sum(l1m[:, b], axis=-1, keepdims=True) for b in blocks]
        logw = []
        for j in reversed(range(len(blocks))):
            logw.append((z[:, blocks[j]] - sp[:, blocks[j]]) + after[j] + run)
            run = run + sums[j]
        a = jnp.exp2(jnp.concatenate(logw[::-1], axis=1))
        if masked:
            a = jnp.where(strict, a, 0.0)
        acc = acc + jnp.dot(a.astype(BF16), v, preferred_element_type=F32)
        return run, acc

    per_q = tq // tk
    carry = (jnp.zeros((tq, 1), F32), jnp.zeros((tq, HEAD_DIM), F32))
    run, acc = span(qi * per_q, per_q, carry, True)

    def live(state):
        block, run, _ = state
        return (block >= 0) & (jnp.max(run) > SB_DEAD_LOG2)

    def earlier(state):
        block, run, acc = state
        run, acc = span(block, 1, (run, acc), False)
        return block - 1, run, acc

    _, _, acc = lax.while_loop(live, earlier, (qi * per_q - 1, run, acc))
    o_ref[...] = acc.astype(o_ref.dtype)


def stick_breaking_attention(qkv, *, batch, seq, q_col, k_col, v_col):
    tq = min(ATT_TQ, seq)
    tk = min(SB_TK, tq)
    nq = seq // tq
    return pl.pallas_call(
        functools.partial(_sb_body, tq=tq, tk=tk, scale=HEAD_DIM ** -0.5),
        grid=(batch, N_HEADS, nq),
        in_specs=[pl.BlockSpec((tq, HEAD_DIM), lambda b, h, i: (b * nq + i, q_col(h))),
                  pl.BlockSpec((seq, HEAD_DIM), lambda b, h, i: (b, k_col(h))),
                  pl.BlockSpec((seq, HEAD_DIM), lambda b, h, i: (b, v_col(h)))],
        out_specs=pl.BlockSpec((tq, HEAD_DIM), lambda b, h, i: (b * nq + i, h)),
        out_shape=jax.ShapeDtypeStruct((batch * seq, N_HEADS * HEAD_DIM), BF16),
        name="stick_breaking",
    )(qkv, qkv, qkv)


def _mla_prep_body(q_ref, kr_ref, kv_ref, cos_ref, sin_lo_ref, sin_hi_ref, qf_ref, kf_ref):
    cos, sin_lo, sin_hi = cos_ref[...], sin_lo_ref[...], sin_hi_ref[...]

    def rope(x):
        return x * cos + pltpu.roll(x, LANE - ROPE_DIM // 2, 1) * sin_lo \
            + pltpu.roll(x, ROPE_DIM // 2, 1) * sin_hi

    k_pe = rope(kr_ref[...]).astype(BF16)
    for h in range(N_HEADS):
        lo = h * MLA_QK
        qf_ref[:, lo:lo + HEAD_DIM] = q_ref[:, lo:lo + HEAD_DIM].astype(BF16)
        qf_ref[:, lo + HEAD_DIM:lo + MLA_QK] = rope(q_ref[:, lo + HEAD_DIM:lo + MLA_QK]).astype(BF16)
        kf_ref[:, lo:lo + HEAD_DIM] = kv_ref[:, lo:lo + HEAD_DIM]
        kf_ref[:, lo + HEAD_DIM:lo + MLA_QK] = k_pe


def mla_prep(q_raw, lat, kv, tables, *, seq):
    m = q_raw.shape[0]
    tm = min(ROW_TILE, seq)
    npos = seq // tm
    wide = N_HEADS * MLA_QK
    kr_block = (Q_LORA + KV_LORA) // LANE
    tab_spec = pl.BlockSpec((tm, LANE), lambda i: (i % npos, 0))
    return pl.pallas_call(
        _mla_prep_body,
        grid=(m // tm,),
        in_specs=[pl.BlockSpec((tm, wide), lambda i: (i, 0)),
                  pl.BlockSpec((tm, LANE), lambda i: (i, kr_block)),
                  pl.BlockSpec((tm, wide), lambda i: (i, 0)),
                  tab_spec, tab_spec, tab_spec],
        out_specs=[pl.BlockSpec((tm, wide), lambda i: (i, 0)),
                   pl.BlockSpec((tm, wide), lambda i: (i, 0))],
        out_shape=[jax.ShapeDtypeStruct((m, wide), BF16),
                   jax.ShapeDtypeStruct((m, wide), BF16)],
        name="mla_prep",
    )(q_raw, lat, kv, *tables)


def rope_tables(seq):
    half = ROPE_DIM // 2
    inv_freq = ROPE_THETA ** (-jnp.arange(half, dtype=F32) / half)
    ang = jnp.arange(seq, dtype=F32)[:, None] * inv_freq[None, :]
    cos, sin = jnp.cos(ang), jnp.sin(ang)
    zeros = jnp.zeros((seq, half), F32)
    pad = jnp.zeros((seq, LANE - ROPE_DIM), F32)
    cos_t = jnp.concatenate([cos, cos, pad], axis=1)
    sin_lo = jnp.concatenate([-sin, zeros, pad], axis=1)
    sin_hi = jnp.concatenate([zeros, sin, pad], axis=1)
    return cos_t, sin_lo, sin_hi


def _dilated_body(q_ref, kp_ref, kc_ref, vp_ref, vc_ref, o_ref, lse_ref, *, rate, scale):
    ib = pl.program_id(2)
    n = DIL_BLOCK
    qi = lax.broadcasted_iota(jnp.int32, (n, 2 * n), 0)
    ki = lax.broadcasted_iota(jnp.int32, (n, 2 * n), 1)
    steps = n + qi - ki
    valid = (steps >= 0) & (steps <= n) & ((ki >= n) | (ib > 0))
    dist = (steps * rate).astype(F32)
    lane = lax.broadcasted_iota(jnp.int32, (n, LANE), 1)
    lse_all = jnp.zeros((n, LANE), F32)
    for h in range(N_HEADS):
        cols = slice(h * HEAD_DIM, (h + 1) * HEAD_DIM)
        q = q_ref[:, cols]
        k = jnp.concatenate([kp_ref[:, cols], kc_ref[:, cols]], axis=0)
        v = jnp.concatenate([vp_ref[:, cols], vc_ref[:, cols]], axis=0)
        s = lax.dot_general(q, k, (((1,), (1,)), ((), ())), preferred_element_type=F32) * scale
        s = jnp.where(valid, s - (2.0 ** -(h + 1)) * dist, NEG_INF)
        m = jnp.max(s, axis=-1, keepdims=True)
        p = jnp.exp(s - m)
        l = jnp.sum(p, axis=-1, keepdims=True)
        o = jnp.dot(p.astype(BF16), v, preferred_element_type=F32) / l
        o_ref[:, cols] = o.astype(o_ref.dtype)
        lse_all = jnp.where(lane == h, m + jnp.log(l), lse_all)
    lse_ref[...] = lse_all


def _dilated_proj_body(a_ref, w_ref, *refs):
    out_refs, acc_ref = refs[:-1], refs[-1]
    acc = jnp.dot(a_ref[...], w_ref[...], preferred_element_type=F32)
    n_slabs, tm, _ = acc_ref.shape
    for j in range(n_slabs):
        acc_ref[j] = acc[:, j * LANE:(j + 1) * LANE]
    for rate, o_ref in zip(DIL_RATES, out_refs):
        if rate == 1:
            o_ref[...] = acc.astype(o_ref.dtype)
            continue
        for c in range(rate):
            for j in range(n_slabs):
                o_ref[c, :, j * LANE:(j + 1) * LANE] = (
                    acc_ref[j, pl.ds(c, tm // rate, stride=rate), :].astype(o_ref.dtype))


def dilated_proj(a, w, *, layer):
    m, k = a.shape
    n = w.shape[2]
    tm, tn = min(MM_TILE, m), MM_TILE

    def out_spec(rate):
        if rate == 1:
            return pl.BlockSpec((tm, tn), lambda i, j: (i, j))
        return pl.BlockSpec((rate, tm // rate, tn), lambda i, j: (0, i, j))

    def out_shape(rate):
        return jax.ShapeDtypeStruct((m, n) if rate == 1 else (rate, m // rate, n), BF16)

    return pl.pallas_call(
        _dilated_proj_body,
        grid=(m // tm, n // tn),
        in_specs=[pl.BlockSpec((tm, k), lambda i, j: (i, 0)),
                  pl.BlockSpec((None, k, tn), lambda i, j: (layer, 0, j))],
        out_specs=[out_spec(r) for r in DIL_RATES],
        out_shape=[out_shape(r) for r in DIL_RATES],
        scratch_shapes=[pltpu.VMEM((tn // LANE, tm, LANE), F32)],
        name="dilated_proj",
    )(a, w)


def dilated_branch(dqkv, *, batch, seq, rate):
    wide = N_HEADS * HEAD_DIM
    n_rows = batch * seq
    nb = seq // rate // DIL_BLOCK
    rows = dqkv.reshape(n_rows, 3 * wide)

    def row_block(b, c, i):
        return (c * batch + b) * nb + i

    def spec(which, prev):
        def index(b, c, i):
            return (row_block(b, c, jnp.maximum(i - 1, 0) if prev else i), which)
        return pl.BlockSpec((DIL_BLOCK, wide), index)

    o, lse = pl.pallas_call(
        functools.partial(_dilated_body, rate=rate, scale=HEAD_DIM ** -0.5),
        grid=(batch, rate, nb),
        in_specs=[spec(0, False), spec(1, True), spec(1, False), spec(2, True), spec(2, False)],
        out_specs=[pl.BlockSpec((DIL_BLOCK, wide), lambda b, c, i: (row_block(b, c, i), 0)),
                   pl.BlockSpec((DIL_BLOCK, LANE), lambda b, c, i: (row_block(b, c, i), 0))],
        out_shape=[jax.ShapeDtypeStruct((n_rows, wide), BF16),
                   jax.ShapeDtypeStruct((n_rows, LANE), F32)],
        name="dilated_branch",
    )(rows, rows, rows, rows, rows)
    if rate == 1:
        return o, lse
    return o.reshape(rate, n_rows // rate, wide), lse.reshape(rate, n_rows // rate, LANE)


def _dilated_merge_body(*refs):
    n = len(DIL_RATES)
    o_refs, l_refs, out_ref = refs[:n], refs[n:2 * n], refs[2 * n]
    scratch = refs[2 * n + 1:]
    outs, lses = [], []
    for g, rate in enumerate(DIL_RATES):
        if rate == 1:
            outs.append(o_refs[g][...].astype(F32))
            lses.append(l_refs[g][...])
            continue
        o_nat, l_nat = scratch[2 * (g - 1)], scratch[2 * (g - 1) + 1]
        rows = l_nat.shape[0] // rate
        for c in range(rate):
            for h in range(N_HEADS):
                o_nat[h, pl.ds(c, rows, stride=rate), :] = (
                    o_refs[g][c, :, h * HEAD_DIM:(h + 1) * HEAD_DIM].astype(F32))
            l_nat[pl.ds(c, rows, stride=rate), :] = l_refs[g][c]
        outs.append(o_nat)
        lses.append(l_nat[...])
    top = functools.reduce(jnp.maximum, lses)
    es = [jnp.exp(x - top) for x in lses]
    den = functools.reduce(lambda a, b: a + b, es)
    ws = [e / den for e in es]
    for h in range(N_HEADS):
        cols = slice(h * HEAD_DIM, (h + 1) * HEAD_DIM)
        acc = ws[0][:, h:h + 1] * outs[0][:, cols]
        for g in range(1, n):
            acc = acc + ws[g][:, h:h + 1] * outs[g][h]
        out_ref[:, cols] = acc.astype(out_ref.dtype)


def dilated_merge(outs, lses):
    m = outs[0].shape[0]
    wide = N_HEADS * HEAD_DIM
    tm = min(ROW_TILE, m)

    def spec(rate, width):
        if rate == 1:
            return pl.BlockSpec((tm, width), lambda i: (i, 0))
        return pl.BlockSpec((rate, tm // rate, width), lambda i: (0, i, 0))

    scratch = []
    for rate in DIL_RATES[1:]:
        scratch += [pltpu.VMEM((N_HEADS, tm, HEAD_DIM), F32), pltpu.VMEM((tm, LANE), F32)]
    return pl.pallas_call(
        _dilated_merge_body,
        grid=(m // tm,),
        in_specs=[spec(r, wide) for r in DIL_RATES] + [spec(r, LANE) for r in DIL_RATES],
        out_specs=pl.BlockSpec((tm, wide), lambda i: (i, 0)),
        out_shape=jax.ShapeDtypeStruct((m, wide), BF16),
        scratch_shapes=scratch,
        name="dilated_merge",
    )(*outs, *lses)


def _swiglu_partial(x, wg, wu, wd):
    g = jnp.dot(x, wg, preferred_element_type=F32)
    u = jnp.dot(x, wu, preferred_element_type=F32)
    mid = (g * (1.0 / (1.0 + jnp.exp(-g))) * u).astype(BF16)
    return jnp.dot(mid, wd, preferred_element_type=F32)


def _dense_ffn_body(x_ref, wg_ref, wu_ref, wd_ref, res_ref, g_ref, o_ref, on_ref):
    f = pl.program_id(1)

    @pl.when(f == 0)
    def _():
        o_ref[...] = res_ref[...]

    o_ref[...] += _swiglu_partial(x_ref[...], wg_ref[...], wu_ref[...], wd_ref[...])

    @pl.when(f == pl.num_programs(1) - 1)
    def _():
        on_ref[...] = _rms(o_ref[...], g_ref[...]).astype(on_ref.dtype)


def dense_ffn(x, w_gate, w_up, w_down, res, g_next, *, layer):
    m, d = x.shape
    ff = w_gate.shape[2]
    tm, tf = min(FFN_TM, m), min(FFN_TF, ff)
    row = pl.BlockSpec((tm, d), lambda i, f: (i, 0))
    return pl.pallas_call(
        _dense_ffn_body,
        grid=(m // tm, ff // tf),
        in_specs=[row,
                  pl.BlockSpec((None, d, tf), lambda i, f: (layer, 0, f)),
                  pl.BlockSpec((None, d, tf), lambda i, f: (layer, 0, f)),
                  pl.BlockSpec((None, tf, d), lambda i, f: (layer, f, 0)),
                  row,
                  pl.BlockSpec((1, d), lambda i, f: (0, 0))],
        out_specs=[row, row],
        out_shape=[jax.ShapeDtypeStruct((m, d), F32), jax.ShapeDtypeStruct((m, d), BF16)],
        name="dense_ffn",
    )(x, w_gate, w_up, w_down, res, g_next.reshape(1, d).astype(F32))


def _pack_bf16_pairs(lo, hi):
    lo_bits = lax.bitcast_convert_type(lo.astype(BF16).astype(F32), jnp.uint32)
    hi_bits = lax.bitcast_convert_type(hi.astype(BF16).astype(F32), jnp.uint32)
    return (lo_bits >> 16) | (hi_bits & jnp.uint32(0xFFFF0000))


def _unpack_bf16_pairs(packed):
    lo = lax.bitcast_convert_type(packed << 16, F32).astype(BF16)
    hi = lax.bitcast_convert_type(packed & jnp.uint32(0xFFFF0000), F32).astype(BF16)
    return lo, hi


def _grouped_ffn_body(meta_ref, x_ref, wg_ref, wu_ref, wd_ref, o_ref, xb_ref, *, n_tiles, sub):
    i = pl.program_id(0)
    f = pl.program_id(1)
    in_use = i < meta_ref[n_tiles]
    rows = meta_ref[n_tiles + 1 + i]
    half = x_ref.shape[1]

    @pl.when(f == 0)
    def _():
        o_ref[...] = jnp.zeros_like(o_ref)

    @pl.when(in_use & (f == 0))
    def _():
        lo, hi = _unpack_bf16_pairs(x_ref[...])
        xb_ref[:, :half] = lo
        xb_ref[:, half:] = hi

    tm = x_ref.shape[0]
    half_sub = sub // 2

    def run_blocks(blocks):
        wg, wu, wd = (w[...].astype(BF16) for w in (wg_ref, wu_ref, wd_ref))
        for start, size in blocks:
            sl = pl.ds(start, size)
            o_ref[sl, :] += _swiglu_partial(xb_ref[sl, :], wg, wu, wd)

    for n_half in range(1, tm // half_sub + 1):
        full, tail = divmod(n_half, 2)
        blocks = [(b * sub, sub) for b in range(full)] + [(full * sub, half_sub)] * tail
        lo, hi = (n_half - 1) * half_sub, n_half * half_sub
        pl.when(in_use & (rows > lo) & (rows <= hi))(functools.partial(run_blocks, blocks))


def grouped_ffn(xs, meta, w_gate, w_up, w_down, *, layer, tm):
    ns, half = xs.shape
    d = 2 * half
    n_tiles = ns // tm
    ff = w_gate.shape[3]
    tf = min(MOE_TF, ff)
    nf = ff // tf

    def tile(i, meta):
        return jnp.maximum(jnp.minimum(i, meta[n_tiles] - 1), 0)

    def expert(i, meta):
        return meta[tile(i, meta)]

    def chunk(i, f, meta):
        return jnp.where(i < meta[n_tiles], f, nf - 1)

    grid_spec = pltpu.PrefetchScalarGridSpec(
        num_scalar_prefetch=1,
        grid=(n_tiles, nf),
        in_specs=[pl.BlockSpec((tm, half), lambda i, f, meta: (tile(i, meta), 0)),
                  pl.BlockSpec((None, None, d, tf),
                               lambda i, f, meta: (layer, expert(i, meta), 0, chunk(i, f, meta))),
                  pl.BlockSpec((None, None, d, tf),
                               lambda i, f, meta: (layer, expert(i, meta), 0, chunk(i, f, meta))),
                  pl.BlockSpec((None, None, tf, d),
                               lambda i, f, meta: (layer, expert(i, meta), chunk(i, f, meta), 0))],
        out_specs=pl.BlockSpec((tm, d), lambda i, f, meta: (i, 0)),
        scratch_shapes=[pltpu.VMEM((tm, d), BF16)],
    )
    return pl.pallas_call(
        functools.partial(_grouped_ffn_body, n_tiles=n_tiles, sub=min(MOE_SUB, tm)),
        grid_spec=grid_spec,
        out_shape=jax.ShapeDtypeStruct((ns, d), F32),
        name="grouped_ffn",
    )(meta, xs, w_gate, w_up, w_down)


ROUTE_I1, ROUTE_I2, ROUTE_G1, ROUTE_G2, ROUTE_P1, ROUTE_P2 = range(6)


def _router_body(h_ref, g_ref, wr_ref, route_ref, cnt_ref, seen_ref, *, tr):
    @pl.when(pl.program_id(0) == 0)
    def _():
        seen_ref[...] = jnp.zeros_like(seen_ref)

    y = _rms(h_ref[...], g_ref[...])
    y_hi = y.astype(BF16)
    y_lo = (y - y_hi.astype(F32)).astype(BF16)
    w = wr_ref[...]
    w_hi = w.astype(BF16)
    w_lo = (w - w_hi.astype(F32)).astype(BF16)

    def dot_nt(a, b):
        return lax.dot_general(a, b, (((1,), (1,)), ((), ())), preferred_element_type=F32)

    logits = dot_nt(w_hi, y_hi) + (dot_nt(w_hi, y_lo) + dot_nt(w_lo, y_hi))
    expert = lax.broadcasted_iota(jnp.int32, (N_EXPERTS, tr), 0)
    m1 = jnp.max(logits, axis=0, keepdims=True)
    i1 = jnp.min(jnp.where(logits == m1, expert, N_EXPERTS), axis=0, keepdims=True)
    rest = jnp.where(expert == i1, NEG_INF, logits)
    m2 = jnp.max(rest, axis=0, keepdims=True)
    i2 = jnp.min(jnp.where(rest == m2, expert, N_EXPERTS), axis=0, keepdims=True)
    e = jnp.exp(m2 - m1)
    g1 = 1.0 / (1.0 + e)
    g2 = e / (1.0 + e)
    sel = jnp.where((expert == i1) | (expert == i2), 1.0, 0.0)
    r = lax.broadcasted_iota(jnp.int32, (tr, tr), 0)
    c = lax.broadcasted_iota(jnp.int32, (tr, tr), 1)
    earlier = jnp.where(r < c, 1.0, 0.0).astype(BF16)
    seen = seen_ref[...]
    rank = jnp.dot(sel.astype(BF16), earlier, preferred_element_type=F32) + seen[:, 0:1]
    p1 = jnp.sum(jnp.where(expert == i1, rank, 0.0), axis=0, keepdims=True)
    p2 = jnp.sum(jnp.where(expert == i2, rank, 0.0), axis=0, keepdims=True)
    seen = seen + jnp.sum(sel, axis=1, keepdims=True)
    seen_ref[...] = seen
    cnt_ref[...] = seen
    route = jnp.zeros((SUBLANE, tr), F32)
    row = lax.broadcasted_iota(jnp.int32, (SUBLANE, tr), 0)
    for idx, val in ((ROUTE_I1, i1.astype(F32)), (ROUTE_I2, i2.astype(F32)), (ROUTE_G1, g1),
                     (ROUTE_G2, g2), (ROUTE_P1, p1), (ROUTE_P2, p2)):
        route = jnp.where(row == idx, val, route)
    route_ref[...] = route


def router(h, g, w_router):
    assert N_EXPERTS == SUBLANE
    m, d = h.shape
    tr = min(ROW_TILE, m)
    return pl.pallas_call(
        functools.partial(_router_body, tr=tr),
        grid=(m // tr,),
        in_specs=[pl.BlockSpec((tr, d), lambda i: (i, 0)),
                  pl.BlockSpec((1, d), lambda i: (0, 0)),
                  pl.BlockSpec((N_EXPERTS, d), lambda i: (0, 0))],
        out_specs=[pl.BlockSpec((SUBLANE, tr), lambda i: (0, i)),
                   pl.BlockSpec((N_EXPERTS, LANE), lambda i: (0, 0))],
        out_shape=[jax.ShapeDtypeStruct((SUBLANE, m), F32),
                   jax.ShapeDtypeStruct((N_EXPERTS, LANE), F32)],
        scratch_shapes=[pltpu.VMEM((N_EXPERTS, LANE), F32)],
        name="router",
    )(h, g.reshape(1, d).astype(F32), w_router.astype(F32).T)


def _row_copies(copies, step, slot, tr, wait):
    def body(r, _):
        for queue, cp in enumerate(copies(step, slot, r)):
            if wait:
                cp.wait()
            else:
                cp.start(priority=queue)
        return 0

    lax.fori_loop(0, tr, body, 0, unroll=DMA_UNROLL)


def _scatter_body(s1_ref, s2_ref, h_ref, g_ref, xs_in_ref, xs_ref, buf_ref, sems, *, tr):
    del xs_in_ref
    i = pl.program_id(0)
    slot = i % 2
    y = _rms(h_ref[...], g_ref[...])
    half = y.shape[1] // 2
    buf_ref[slot] = _pack_bf16_pairs(y[:, :half], y[:, half:])

    def copies(step, slot, r):
        src = buf_ref.at[slot, pl.ds(r, 1)]
        t = step * tr + r
        return (pltpu.make_async_copy(src, xs_ref.at[pl.ds(s1_ref[t], 1)], sems.at[slot]),
                pltpu.make_async_copy(src, xs_ref.at[pl.ds(s2_ref[t], 1)], sems.at[slot]))

    _row_copies(copies, i, slot, tr, wait=False)

    @pl.when(i > 0)
    def _():
        _row_copies(copies, i - 1, 1 - slot, tr, wait=True)

    @pl.when(i == pl.num_programs(0) - 1)
    def _():
        _row_copies(copies, i, slot, tr, wait=True)


def scatter_rows(h, g, slot1, slot2, n_slots):
    m, d = h.shape
    tr = min(DMA_ROWS, m)
    grid_spec = pltpu.PrefetchScalarGridSpec(
        num_scalar_prefetch=2,
        grid=(m // tr,),
        in_specs=[pl.BlockSpec((tr, d), lambda i, s1, s2: (i, 0)),
                  pl.BlockSpec((1, d), lambda i, s1, s2: (0, 0)),
                  pl.BlockSpec(memory_space=pl.ANY)],
        out_specs=pl.BlockSpec(memory_space=pl.ANY),
        scratch_shapes=[pltpu.VMEM((2, tr, d // 2), jnp.uint32), pltpu.SemaphoreType.DMA((2,))],
    )
    return pl.pallas_call(
        functools.partial(_scatter_body, tr=tr),
        grid_spec=grid_spec,
        out_shape=jax.ShapeDtypeStruct((n_slots, d // 2), jnp.uint32),
        input_output_aliases={4: 0},
        compiler_params=pltpu.CompilerParams(dimension_semantics=("arbitrary",)),
        name="scatter_rows",
    )(slot1, slot2, h, g.reshape(1, d).astype(F32), jnp.zeros((n_slots, d // 2), jnp.uint32))


def _combine_body(s1_ref, s2_ref, h_ref, gate_ref, g_ref, y_ref, o_ref, on_ref, buf_ref, sems, *, tr):
    i = pl.program_id(0)
    slot = i % 2

    def copies(step, slot, r):
        t = step * tr + r
        return (pltpu.make_async_copy(y_ref.at[pl.ds(s1_ref[t], 1)], buf_ref.at[slot, 0, pl.ds(r, 1)],
                                      sems.at[slot]),
                pltpu.make_async_copy(y_ref.at[pl.ds(s2_ref[t], 1)], buf_ref.at[slot, 1, pl.ds(r, 1)],
                                      sems.at[slot]))

    @pl.when(i == 0)
    def _():
        _row_copies(copies, 0, 0, tr, wait=False)

    @pl.when(i + 1 < pl.num_programs(0))
    def _():
        _row_copies(copies, i + 1, 1 - slot, tr, wait=False)

    _row_copies(copies, i, slot, tr, wait=True)
    gates = gate_ref[...]
    h = h_ref[...] + (gates[:, 0:1] * buf_ref[slot, 0] + gates[:, 1:2] * buf_ref[slot, 1])
    o_ref[...] = h
    on_ref[...] = _rms(h, g_ref[...]).astype(on_ref.dtype)


def combine_rows(h, gates, y, slot1, slot2, g_next, norm_dtype):
    m, d = h.shape
    tr = min(DMA_ROWS, m)
    row = pl.BlockSpec((tr, d), lambda i, s1, s2: (i, 0))
    grid_spec = pltpu.PrefetchScalarGridSpec(
        num_scalar_prefetch=2,
        grid=(m // tr,),
        in_specs=[row,
                  pl.BlockSpec((tr, LANE), lambda i, s1, s2: (i, 0)),
                  pl.BlockSpec((1, d), lambda i, s1, s2: (0, 0)),
                  pl.BlockSpec(memory_space=pl.ANY)],
        out_specs=[row, row],
        scratch_shapes=[pltpu.VMEM((2, 2, tr, d), F32), pltpu.SemaphoreType.DMA((2,))],
    )
    return pl.pallas_call(
        functools.partial(_combine_body, tr=tr),
        grid_spec=grid_spec,
        out_shape=[jax.ShapeDtypeStruct((m, d), F32), jax.ShapeDtypeStruct((m, d), norm_dtype)],
        compiler_params=pltpu.CompilerParams(dimension_semantics=("arbitrary",)),
        name="combine_rows",
    )(slot1, slot2, h, gates, g_next.reshape(1, d).astype(F32), y)


def moe_ffn(h, g, w_router, w_gate, w_up, w_down, g_next, norm_dtype, *, layer):
    m, _ = h.shape
    tm = min(MOE_TM, m)
    n_tiles = -(-(2 * m) // tm) + N_EXPERTS
    route, counts = router(h, g, w_router)
    col = lambda j: route[j].astype(jnp.int32)
    cnt = counts[:, 0].astype(jnp.int32)
    tiles_per = (cnt + tm - 1) // tm
    tile_end = jnp.cumsum(tiles_per)
    tile_start = tile_end - tiles_per
    group_start = tile_start * tm
    slot1 = group_start[col(ROUTE_I1)] + col(ROUTE_P1)
    slot2 = group_start[col(ROUTE_I2)] + col(ROUTE_P2)
    tile_ids = jnp.arange(n_tiles)
    tile_expert = jnp.minimum(jnp.sum(tile_ids[:, None] >= tile_end[None, :], axis=1), N_EXPERTS - 1)
    tile_rows = jnp.clip(cnt[tile_expert] - (tile_ids - tile_start[tile_expert]) * tm, 0, tm)
    meta = jnp.concatenate([tile_expert, tile_end[-1:], tile_rows]).astype(jnp.int32)
    xs = scatter_rows(h, g, slot1, slot2, n_tiles * tm)
    ys = grouped_ffn(xs, meta, w_gate, w_up, w_down, layer=layer, tm=tm)
    gates = jnp.zeros((m, LANE), F32).at[:, :2].set(route[ROUTE_G1:ROUTE_G2 + 1].T)
    return combine_rows(h, gates, ys, slot1, slot2, g_next, norm_dtype)


def prep_even(p):
    wide = N_HEADS * HEAD_DIM
    w_in = p["w_in"]
    n_layers, d, _ = w_in.shape
    w_fg = jnp.zeros((n_layers, d, LANE), BF16).at[:, :, :N_HEADS].set(w_in[:, :, 6 * wide:].astype(BF16))
    return dict(p, w_qkv=w_in[:, :, :6 * wide].astype(BF16), w_fg=w_fg, w_o=p["w_o"].astype(BF16),
                w_gate=p["w_gate"].astype(BF16), w_up=p["w_up"].astype(BF16),
                w_down=p["w_down"].astype(BF16))


def even_layer(h, hn, p, i, g_next, *, batch, seq):
    qkv = matmul([hn], [(p["w_qkv"], 0)], BF16, layer=i)
    fg = matmul([hn], [(p["w_fg"], 0)], F32, layer=i)
    c = decay_cumsum(fg, p["b_f"][i], batch, seq)
    c_row = c[:, :N_HEADS].reshape(batch, seq, N_HEADS).transpose(0, 2, 1).reshape(batch, N_HEADS, 1, seq)
    o_a = flash_attention(qkv, qkv, qkv, batch=batch, seq=seq, dqk=HEAD_DIM,
                          q_col=lambda hd: hd, k_col=lambda hd: N_HEADS + hd,
                          v_col=lambda hd: 2 * N_HEADS + hd, scale=HEAD_DIM ** -0.5,
                          decay=(c, c_row))
    o_b = stick_breaking_attention(qkv, batch=batch, seq=seq, q_col=lambda hd: 3 * N_HEADS + hd,
                                   k_col=lambda hd: 4 * N_HEADS + hd,
                                   v_col=lambda hd: 5 * N_HEADS + hd)
    h, hn = out_proj_norm([o_a, o_b], p["w_o"], h, p["norm_ffn"][i], layer=i)
    return dense_ffn(hn, p["w_gate"], p["w_up"], p["w_down"], h, g_next, layer=i)


def prep_odd(p):
    w_in = p["w_in"]
    n_layers, d, _ = w_in.shape
    lat_w = Q_LORA + KV_LORA
    w_lat = jnp.zeros((n_layers, d, lat_w + LANE), BF16).at[:, :, :lat_w + ROPE_DIM].set(
        w_in[:, :, :lat_w + ROPE_DIM].astype(BF16))
    w_uq = jnp.zeros((n_layers, Q_LORA, N_HEADS, MLA_QK), BF16).at[:, :, :, :HEAD_DIM + ROPE_DIM].set(
        p["w_uq"].reshape(n_layers, Q_LORA, N_HEADS, HEAD_DIM + ROPE_DIM).astype(BF16))
    return dict(p, w_lat=w_lat, w_dil=w_in[:, :, lat_w + ROPE_DIM:].astype(BF16),
                w_uq=w_uq.reshape(n_layers, Q_LORA, N_HEADS * MLA_QK),
                w_ukv=p["w_ukv"].astype(BF16), w_o=p["w_o"].astype(BF16))


def odd_layer(h, hn, p, i, tables, g_next, norm_dtype, *, batch, seq):
    lat = matmul([hn], [(p["w_lat"], 0)], F32, layer=i)
    dqkv = dilated_proj(hn, p["w_dil"], layer=i)
    cqn = rmsnorm(lat, p["g_cq"][i], BF16, col_block=0, width=Q_LORA)
    ckvn = rmsnorm(lat, p["g_ckv"][i], BF16, col_block=1, width=KV_LORA)
    q_raw = matmul([cqn], [(p["w_uq"], 0)], F32, layer=i)
    kv = matmul([ckvn], [(p["w_ukv"], 0)], BF16, layer=i)
    q_full, k_full = mla_prep(q_raw, lat, kv, tables, seq=seq)
    o_c = flash_attention(q_full, k_full, kv, batch=batch, seq=seq, dqk=MLA_QK,
                          q_col=lambda hd: hd, k_col=lambda hd: hd, v_col=lambda hd: 2 * hd + 1,
                          scale=MLA_SCALE_DIM ** -0.5)
    branches = [dilated_branch(x, batch=batch, seq=seq, rate=r) for x, r in zip(dqkv, DIL_RATES)]
    o_d = dilated_merge([o for o, _ in branches], [l for _, l in branches])
    h = matmul([o_c, o_d], [(p["w_o"], 0), (p["w_o"], 1)], F32, res=h, layer=i)
    return moe_ffn(h, p["norm_ffn"][i], p["w_router"][i], p["w_exp_gate"], p["w_exp_up"],
                   p["w_exp_down"], g_next, norm_dtype, layer=i)


def kernel(x, even_norm_mix, even_w_in, even_b_f, even_w_o, even_norm_ffn, even_w_gate, even_w_up,
           even_w_down, odd_norm_mix, odd_w_in, odd_g_cq, odd_g_ckv, odd_w_uq, odd_w_ukv, odd_w_o,
           odd_norm_ffn, odd_w_router, odd_w_exp_gate, odd_w_exp_up, odd_w_exp_down, final_norm):
    batch, seq, d = x.shape
    assert seq % (max(DIL_RATES) * DIL_BLOCK) == 0
    even = prep_even(dict(norm_mix=even_norm_mix, w_in=even_w_in, b_f=even_b_f, w_o=even_w_o,
                          norm_ffn=even_norm_ffn, w_gate=even_w_gate, w_up=even_w_up,
                          w_down=even_w_down))
    odd = prep_odd(dict(norm_mix=odd_norm_mix, w_in=odd_w_in, g_cq=odd_g_cq, g_ckv=odd_g_ckv,
                        w_uq=odd_w_uq, w_ukv=odd_w_ukv, w_o=odd_w_o, norm_ffn=odd_norm_ffn,
                        w_router=odd_w_router, w_exp_gate=odd_w_exp_gate, w_exp_up=odd_w_exp_up,
                        w_exp_down=odd_w_exp_down))
    depth = even_w_in.shape[0] + odd_w_in.shape[0]
    tables = rope_tables(seq)
    assert depth % 2 == 0, "the final norm is fused into the last odd layer"
    h = x.reshape(batch * seq, d)
    hn = rmsnorm(h, even_norm_mix[0], BF16)
    for layer in range(depth):
        i = layer // 2
        if layer % 2 == 0:
            h, hn = even_layer(h, hn, even, i, odd_norm_mix[i], batch=batch, seq=seq)
        elif layer + 1 < depth:
            h, hn = odd_layer(h, hn, odd, i, tables, even_norm_mix[i + 1], BF16, batch=batch, seq=seq)
        else:
            h, hn = odd_layer(h, hn, odd, i, tables, final_norm, x.dtype, batch=batch, seq=seq)
    return hn.reshape(batch, seq, d)
```
